```python
import math
import jax, jax.numpy as jnp
from jax import lax
import numpy as np

D_MODEL = 1024
BATCH = 4
SEQ = 4096
DEPTH = 4

MEM_LEN = 256
HEAD_DIM = 64
EPS = 1e-6
CONV_CH = D_MODEL // 4
MOBA_WIDTH = D_MODEL // 2
N_MOBA_HEADS = MOBA_WIDTH // HEAD_DIM
GLA_WIDTH = D_MODEL // 4
N_GLA_HEADS = 4
GLA_DV = GLA_WIDTH // N_GLA_HEADS
GLA_DK = GLA_DV // 2
GLA_KEY_WIDTH = N_GLA_HEADS * GLA_DK
GLA_GATE_RANK = 16
GLA_GATE_NORMALIZER = 16.0
GLA_CHUNK = 64
MOBA_BLOCK = 256
MOBA_TOPK = 3
MOBA_QCHUNK = 32
CONV_K = 3
N_XATTN_HEADS = 4
XATTN_WIDTH = N_XATTN_HEADS * HEAD_DIM
D_FF = 2752
IN_SPLITS = (CONV_CH, CONV_CH, CONV_CH,
             MOBA_WIDTH, MOBA_WIDTH, MOBA_WIDTH,
             GLA_KEY_WIDTH, GLA_KEY_WIDTH, GLA_WIDTH, GLA_WIDTH, GLA_GATE_RANK)
IN_PROJ_WIDTH = sum(IN_SPLITS)
MIX_WIDTH = CONV_CH + MOBA_WIDTH + GLA_WIDTH

kernel_name = "hybrid_conv_moba_gla_trunk"


def rmsnorm(x, g):
    xf = x.astype(jnp.float32)
    xf = xf * lax.rsqrt(jnp.mean(xf * xf, axis=-1, keepdims=True) + EPS)
    return xf.astype(x.dtype) * g


def causal_dwconv(u, w, b):
    T = u.shape[1]
    up = jnp.pad(u, ((0, 0), (CONV_K - 1, 0), (0, 0)))
    return sum(up[:, j:j + T] * w[j] for j in range(CONV_K)) + b


def alibi_slopes(n_heads):
    return jnp.asarray(2.0 ** (-8.0 * np.arange(1, n_heads + 1) / n_heads), dtype=jnp.float32)


def moba_attention(q, k, v, slopes):
    Bsz, T, H, hd = q.shape
    nblk = -(-T // MOBA_BLOCK)
    Tp = nblk * MOBA_BLOCK
    pad = ((0, 0), (0, Tp - T), (0, 0), (0, 0))
    q = jnp.pad(q, pad).transpose(0, 2, 1, 3) * (hd ** -0.5)
    k = jnp.pad(k, pad).transpose(0, 2, 1, 3)
    v = jnp.pad(v, pad).transpose(0, 2, 1, 3)
    kb = k.reshape(Bsz, H, nblk, MOBA_BLOCK, hd)
    vb = v.reshape(Bsz, H, nblk, MOBA_BLOCK, hd)
    kmean = jnp.mean(kb.astype(jnp.float32), axis=3)
    gate = jnp.einsum('bhtd,bhnd->bhtn', q.astype(jnp.float32), kmean)
    qblk = jnp.arange(Tp) // MOBA_BLOCK
    past = jnp.arange(nblk)[None, :] < qblk[:, None]
    gate = jnp.where(past, gate, -jnp.inf)
    topk = min(MOBA_TOPK, nblk)
    _, gidx = lax.top_k(gate, topk)
    bi = jnp.arange(Bsz)[:, None, None, None]
    hi = jnp.arange(H)[None, :, None, None]
    sl = slopes[None, :, None, None]

    def chunk(c):
        s0 = c * MOBA_QCHUNK
        qc = lax.dynamic_slice_in_dim(q, s0, MOBA_QCHUNK, axis=2)
        idx = lax.dynamic_slice_in_dim(gidx, s0, MOBA_QCHUNK, axis=2)
        qpos = s0 + jnp.arange(MOBA_QCHUNK)
        valid = jnp.arange(topk)[None, :] < (qpos // MOBA_BLOCK)[:, None]
        ksel = kb[bi, hi, idx]
        vsel = vb[bi, hi, idx]
        kpos_sel = idx[..., None] * MOBA_BLOCK + jnp.arange(MOBA_BLOCK)
        s_sel = jnp.einsum('bhqd,bhqnkd->bhqnk', qc, ksel).astype(jnp.float32)
        s_sel = s_sel - sl[..., None] * (qpos[:, None, None] - kpos_sel).astype(jnp.float32)
        s_sel = jnp.where(valid[None, None, :, :, None], s_sel, -jnp.inf)
        own = s0 // MOBA_BLOCK
        kown = lax.dynamic_slice_in_dim(k, own * MOBA_BLOCK, MOBA_BLOCK, axis=2)
        vown = lax.dynamic_slice_in_dim(v, own * MOBA_BLOCK, MOBA_BLOCK, axis=2)
        kpos_own = own * MOBA_BLOCK + jnp.arange(MOBA_BLOCK)
        rel = (qpos[:, None] - kpos_own[None, :])
        s_own = jnp.einsum('bhqd,bhkd->bhqk', qc, kown).astype(jnp.float32)
        s_own = s_own - sl * rel.astype(jnp.float32)
        s_own = jnp.where(rel >= 0, s_own, -jnp.inf)
        scores = jnp.concatenate(
            [s_sel.reshape(Bsz, H, MOBA_QCHUNK, topk * MOBA_BLOCK), s_own], axis=-1)
        p = jax.nn.softmax(scores, axis=-1).astype(v.dtype)
        p_sel = p[..., :topk * MOBA_BLOCK].reshape(Bsz, H, MOBA_QCHUNK, topk, MOBA_BLOCK)
        p_own = p[..., topk * MOBA_BLOCK:]
        return (jnp.einsum('bhqnk,bhqnkd->bhqd', p_sel, vsel)
                + jnp.einsum('bhqk,bhkd->bhqd', p_own, vown))

    nqc = Tp // MOBA_QCHUNK
    out = lax.map(chunk, jnp.arange(nqc))
    out = out.transpose(1, 0, 3, 2, 4).reshape(Bsz, Tp, H * hd)
    return out[:, :T]


def gla_attention(q, k, v, log_a):
    Bsz, T, H, dk = q.shape
    dv = v.shape[-1]
    n = T // GLA_CHUNK

    def to_chunks(a):
        return a.reshape(Bsz, n, GLA_CHUNK, H, a.shape[-1]).transpose(0, 3, 1, 2, 4)

    q = to_chunks(q) * (dk ** -0.5)
    k = to_chunks(k)
    vc = to_chunks(v)
    b = jnp.cumsum(to_chunks(log_a).astype(jnp.float32), axis=3)
    b_last = b[:, :, :, -1:, :]
    qd = q * jnp.exp(b)
    kd = k * jnp.exp(-b)
    kend = k * jnp.exp(b_last - b)
    causal = jnp.tril(jnp.ones((GLA_CHUNK, GLA_CHUNK), dtype=bool))
    A = jnp.where(causal, jnp.einsum('bhncd,bhnsd->bhncs', qd, kd), 0.0)
    o_intra = jnp.einsum('bhncs,bhnse->bhnce', A, vc)
    U = jnp.einsum('bhnsd,bhnse->bhnde', kend, vc)
    decay = jnp.exp(b_last[:, :, :, 0, :])

    def step(S, xs):
        g, u = xs
        return g[..., None] * S + u, S

    S0 = jnp.zeros((Bsz, H, dk, dv), dtype=U.dtype)
    _, S_prev = lax.scan(step, S0, (decay.transpose(2, 0, 1, 3), U.transpose(2, 0, 1, 3, 4)))
    S_prev = S_prev.transpose(1, 2, 0, 3, 4)
    o_inter = jnp.einsum('bhncd,bhnde->bhnce', qd, S_prev)
    o = (o_intra + o_inter).transpose(0, 2, 3, 1, 4).reshape(Bsz, T, H, dv)
    return o.astype(v.dtype)


def hybrid_mixer(h, w_in, sc_conv_w, sc_conv_b, gla_w_gate, gla_b_gate, gla_norm_g, w_out, slopes):
    Bsz, T, _ = h.shape
    proj = h @ w_in
    points = tuple(int(p) for p in np.cumsum(IN_SPLITS)[:-1])
    (c_b, c_c, c_h, m_q, m_k, m_v, g_q, g_k, g_v, g_o, g_lr) = jnp.split(proj, points, axis=-1)
    y_conv = c_b * causal_dwconv(c_c * c_h, sc_conv_w, sc_conv_b)
    hs = lambda a, nh: a.reshape(Bsz, T, nh, -1)
    y_moba = moba_attention(hs(m_q, N_MOBA_HEADS), hs(m_k, N_MOBA_HEADS), hs(m_v, N_MOBA_HEADS), slopes)
    log_a = jax.nn.log_sigmoid((g_lr @ gla_w_gate + gla_b_gate).astype(jnp.float32)) / GLA_GATE_NORMALIZER
    o = gla_attention(hs(g_q, N_GLA_HEADS), hs(g_k, N_GLA_HEADS), hs(g_v, N_GLA_HEADS),
                      hs(log_a, N_GLA_HEADS))
    y_gla = rmsnorm(o, gla_norm_g).reshape(Bsz, T, GLA_WIDTH) * jax.nn.silu(g_o)
    return jnp.concatenate([y_conv, y_moba, y_gla], axis=-1) @ w_out


def memory_cross_attention(h, mem_n, wq, wkv, wo):
    Bsz, T, _ = h.shape
    q = (h @ wq).reshape(Bsz, T, N_XATTN_HEADS, HEAD_DIM)
    k, v = jnp.split(mem_n @ wkv, 2, axis=-1)
    k = k.reshape(Bsz, -1, N_XATTN_HEADS, HEAD_DIM)
    v = v.reshape(Bsz, -1, N_XATTN_HEADS, HEAD_DIM)
    s = jnp.einsum('bthd,bmhd->bhtm', q, k).astype(jnp.float32) * (HEAD_DIM ** -0.5)
    p = jax.nn.softmax(s, axis=-1).astype(v.dtype)
    o = jnp.einsum('bhtm,bmhd->bthd', p, v).reshape(Bsz, T, XATTN_WIDTH)
    return o @ wo


def conv_ffn(h, w_up, conv_w, conv_b, w_down):
    u = causal_dwconv(h @ w_up, conv_w, conv_b)
    a, g = jnp.split(u, 2, axis=-1)
    return (jax.nn.silu(a) * g) @ w_down


def setup_inputs(seed: int = 0) -> dict:
    key = jax.random.key(seed)
    ks = jax.random.split(key, 24)
    f32 = jnp.float32

    def w(k, shape, fan_in):
        return jax.random.normal(k, shape, f32) * (fan_in ** -0.5)

    def gain(k, shape):
        return 1.0 + 0.05 * jax.random.normal(k, shape, f32)

    def bias(k, shape):
        return 0.02 * jax.random.normal(k, shape, f32)

    L = DEPTH
    return {
        "x": jax.random.normal(ks[0], (BATCH, SEQ, D_MODEL), f32),
        "mem": jax.random.normal(ks[1], (BATCH, MEM_LEN, D_MODEL), f32),
        "norm_mix_g": gain(ks[2], (L, D_MODEL)),
        "w_in": w(ks[3], (L, D_MODEL, IN_PROJ_WIDTH), D_MODEL),
        "sc_conv_w": w(ks[4], (L, CONV_K, CONV_CH), CONV_K),
        "sc_conv_b": bias(ks[5], (L, CONV_CH)),
        "gla_w_gate": w(ks[6], (L, GLA_GATE_RANK, GLA_KEY_WIDTH), GLA_GATE_RANK),
        "gla_b_gate": bias(ks[7], (L, GLA_KEY_WIDTH)),
        "gla_norm_g": gain(ks[8], (L, GLA_DV)),
        "w_out": w(ks[9], (L, MIX_WIDTH, D_MODEL), MIX_WIDTH),
        "norm_xattn_g": gain(ks[10], (L, D_MODEL)),
        "norm_mem_g": gain(ks[11], (L, D_MODEL)),
        "xattn_wq": w(ks[12], (L, D_MODEL, XATTN_WIDTH), D_MODEL),
        "xattn_wkv": w(ks[13], (L, D_MODEL, 2 * XATTN_WIDTH), D_MODEL),
        "xattn_wo": w(ks[14], (L, XATTN_WIDTH, D_MODEL), XATTN_WIDTH),
        "norm_ffn_g": gain(ks[15], (L, D_MODEL)),
        "ffn_w_up": w(ks[16], (L, D_MODEL, 2 * D_FF), D_MODEL),
        "ffn_conv_w": w(ks[17], (L, CONV_K, 2 * D_FF), CONV_K),
        "ffn_conv_b": bias(ks[18], (L, 2 * D_FF)),
        "ffn_w_down": w(ks[19], (L, D_FF, D_MODEL), D_FF),
        "final_norm_g": gain(ks[20], (D_MODEL,)),
    }


def reference(x, mem, norm_mix_g, w_in, sc_conv_w, sc_conv_b, gla_w_gate, gla_b_gate, gla_norm_g,
              w_out, norm_xattn_g, norm_mem_g, xattn_wq, xattn_wkv, xattn_wo, norm_ffn_g,
              ffn_w_up, ffn_conv_w, ffn_conv_b, ffn_w_down, final_norm_g):
    slopes = alibi_slopes(N_MOBA_HEADS)
    for l in range(DEPTH):
        h = rmsnorm(x, norm_mix_g[l])
        x = x + hybrid_mixer(h, w_in[l], sc_conv_w[l], sc_conv_b[l], gla_w_gate[l], gla_b_gate[l],
                             gla_norm_g[l], w_out[l], slopes)
        h = rmsnorm(x, norm_xattn_g[l])
        x = x + memory_cross_attention(h, rmsnorm(mem, norm_mem_g[l]), xattn_wq[l], xattn_wkv[l], xattn_wo[l])
        h = rmsnorm(x, norm_ffn_g[l])
        x = x + conv_ffn(h, ffn_w_up[l], ffn_conv_w[l], ffn_conv_b[l], ffn_w_down[l])
    return rmsnorm(x, final_norm_g)
```

```python
import functools

import numpy as np
import jax
import jax.numpy as jnp
from jax import lax
from jax.experimental import pallas as pl
from jax.experimental.pallas import tpu as pltpu

F32 = jnp.float32
BF16 = jnp.bfloat16

EPS = 1e-6
HEAD_DIM = 64
N_MOBA_HEADS = 8
MOBA_BLOCK = 256
MOBA_TOPK = 3
N_GLA_HEADS = 4
GLA_DK = 32
GLA_DV = 64
GLA_CHUNK = 64
GLA_GATE_RANK = 16
GLA_GATE_NORMALIZER = 16.0
CONV_K = 3
N_XATTN_HEADS = 4

LANES = 128
BF16_SUBLANES = 16
VMEM_LIMIT = 56 * 1024 * 1024

MASK_BIAS = -(2.0 ** 60)

_NT = (((1,), (1,)), ((), ()))
_TN = (((0,), (0,)), ((), ()))


def _dot(a, b):
    return jnp.dot(a, b, preferred_element_type=F32)


def _dot_nt(a, b):
    return lax.dot_general(a, b, _NT, preferred_element_type=F32)


def _dot_tn(a, b):
    return lax.dot_general(a, b, _TN, preferred_element_type=F32)


def _split_bf16(a):
    hi = a.astype(BF16)
    lo = (a - hi.astype(F32)).astype(BF16)
    return hi, lo


def _rms(x, g):
    r = lax.rsqrt(jnp.mean(x * x, axis=-1, keepdims=True) + EPS)
    return (x * r) * g


def _params(*sem):
    return pltpu.CompilerParams(dimension_semantics=sem, vmem_limit_bytes=VMEM_LIMIT)


def _resident(shape):
    nd = len(shape)
    return pl.BlockSpec(shape, lambda *_: (0,) * nd, pipeline_mode=pl.Buffered(1))


def _norm_inproj_kernel(x_ref, g_ref, w_ref, *out_refs):
    h = _rms(x_ref[...], g_ref[...]).astype(BF16)
    off = 0
    for o_ref in out_refs:
        wd = o_ref.shape[-1]
        o_ref[...] = _dot(h, w_ref[:, off:off + wd]).astype(o_ref.dtype)
        off += wd


def _norm_inproj(x, g, w, widths, tm):
    n, d = x.shape
    return pl.pallas_call(
        _norm_inproj_kernel,
        grid=(n // tm,),
        in_specs=[pl.BlockSpec((tm, d), lambda i: (i, 0)),
                  _resident((1, d)),
                  _resident(w.shape)],
        out_specs=[pl.BlockSpec((tm, wd), lambda i: (i, 0)) for wd in widths],
        out_shape=[jax.ShapeDtypeStruct((n, wd), BF16) for wd in widths],
        compiler_params=_params("parallel"),
        name="norm_inproj",
    )(x, g, w)


def _moba_kernel(q_ref, k_ref, v_ref, kaux_ref, s256_ref, o_ref,
                 kaug_ref, vt_ref, kmean_ref, *, nblk, blk):
    qb = pl.program_id(2)
    hd2 = 2 * HEAD_DIM

    @pl.when(qb == 0)
    def _prepare():
        lane = lax.broadcasted_iota(jnp.int32, (blk, hd2), 1)
        aux = kaux_ref[0].astype(F32)
        for n in range(nblk):
            rows = slice(n * blk, (n + 1) * blk)
            kb = k_ref[0, rows, :]
            kaug_ref[rows, 0:hd2] = kb
            kaug_ref[rows, hd2:2 * hd2] = (aux + jnp.where(lane == n, 1.0, 0.0)).astype(BF16)
            vt_ref[:, rows] = v_ref[0, rows, :].astype(F32).T.astype(BF16)
            kmean_ref[n:n + 1, :] = jnp.sum(kb.astype(F32), axis=0, keepdims=True) * (1.0 / blk)

    q = q_ref[0]
    qt = q.astype(F32).T
    qt_b = qt.astype(BF16)
    km = kmean_ref[...]
    km_lane_head = lax.broadcasted_iota(jnp.int32, (nblk, hd2), 1) >> 6
    qt_row_head = lax.broadcasted_iota(jnp.int32, (hd2, blk), 0) >> 6
    nidx = lax.broadcasted_iota(jnp.int32, (nblk, blk), 0)
    aux_row = lax.broadcasted_iota(jnp.int32, (hd2 - nblk, blk), 0)
    past = nidx < qb

    qaug = []
    for hh in range(2):
        kmh = jnp.where(km_lane_head == hh, km, 0.0)
        kh, kl = _split_bf16(kmh)
        gate = _dot(kh, qt_b) + _dot(kl, qt_b)
        g = jnp.where(past, gate, -jnp.inf)
        rank = jnp.zeros((nblk, blk), jnp.int32)
        for m in range(nblk):
            gm = g[m:m + 1, :]
            beats = (gm > g) | ((gm == g) & (m < nidx))
            rank = rank + beats.astype(jnp.int32)
        sel = (past & (rank < MOBA_TOPK)) | (nidx == qb)
        s256 = s256_ref[0, hh:hh + 1, :]
        bias = jnp.where(sel, (nidx - qb).astype(F32) * s256, MASK_BIAS)
        head_one = jnp.where(aux_row == hh, 1.0, 0.0)
        qaug.append(jnp.concatenate(
            [jnp.where(qt_row_head == hh, qt, 0.0), bias, head_one], axis=0).astype(BF16))

    key_i = lax.broadcasted_iota(jnp.int32, (blk, blk), 0)
    qry_i = lax.broadcasted_iota(jnp.int32, (blk, blk), 1)
    causal = key_i <= qry_i

    def scores(n, hh):
        off = pl.multiple_of(n * blk, blk)
        st = _dot(kaug_ref[pl.ds(off, blk), :], qaug[hh])
        vth = vt_ref[hh * HEAD_DIM:(hh + 1) * HEAD_DIM, pl.ds(off, blk)]
        return st, vth

    carry = []
    for hh in range(2):
        st, vth = scores(qb, hh)
        st = jnp.where(causal, st, -jnp.inf)
        m = jnp.max(st, axis=0, keepdims=True)
        p = jnp.exp(st - m)
        l = jnp.sum(p, axis=0, keepdims=True)
        acc = _dot(vth, p.astype(BF16))
        carry += [m, l, acc]

    def body(n, c):
        out = []
        for hh in range(2):
            m, l, acc = c[3 * hh:3 * hh + 3]
            st, vth = scores(n, hh)
            mn = jnp.maximum(m, jnp.max(st, axis=0, keepdims=True))
            alpha = jnp.exp(m - mn)
            p = jnp.exp(st - mn)
            l = alpha * l + jnp.sum(p, axis=0, keepdims=True)
            acc = alpha * acc + _dot(vth, p.astype(BF16))
            out += [mn, l, acc]
        return tuple(out)

    c = lax.fori_loop(0, qb, body, tuple(carry))
    ot = jnp.concatenate([c[2] * (1.0 / c[1]), c[5] * (1.0 / c[4])], axis=0)
    o_ref[0] = ot.T.astype(o_ref.dtype)


def _moba_tables(nblk):
    slopes = 2.0 ** (-8.0 * np.arange(1, N_MOBA_HEADS + 1) / N_MOBA_HEADS)
    npair = N_MOBA_HEADS // 2
    kaux = np.zeros((npair, MOBA_BLOCK, 2 * HEAD_DIM), np.float32)
    s256 = np.zeros((npair, 8, MOBA_BLOCK), np.float32)
    for p in range(npair):
        for hh in range(2):
            kaux[p, :, nblk + hh] = slopes[2 * p + hh] * np.arange(MOBA_BLOCK)
            s256[p, hh, :] = slopes[2 * p + hh] * MOBA_BLOCK
    return jnp.asarray(kaux, BF16), jnp.asarray(s256, F32)


def _moba(q, k, v):
    b, t, w = q.shape
    blk = MOBA_BLOCK
    nblk = t // blk
    assert t % blk == 0 and nblk % 8 == 0 and nblk <= HEAD_DIM
    hd2 = 2 * HEAD_DIM
    npair = w // hd2
    kaux, s256 = _moba_tables(nblk)
    kv_spec = pl.BlockSpec((1, t, hd2), lambda bi, p, i: (bi, 0, p))
    return pl.pallas_call(
        functools.partial(_moba_kernel, nblk=nblk, blk=blk),
        grid=(b, npair, nblk),
        in_specs=[pl.BlockSpec((1, blk, hd2), lambda bi, p, i: (bi, i, p)),
                  kv_spec, kv_spec,
                  pl.BlockSpec((1, blk, hd2), lambda bi, p, i: (p, 0, 0)),
                  pl.BlockSpec((1, 8, blk), lambda bi, p, i: (p, 0, 0))],
        out_specs=pl.BlockSpec((1, blk, hd2), lambda bi, p, i: (bi, i, p)),
        out_shape=jax.ShapeDtypeStruct((b, t, w), BF16),
        scratch_shapes=[pltpu.VMEM((t, 2 * hd2), BF16),
                        pltpu.VMEM((hd2, t), BF16),
                        pltpu.VMEM((nblk, hd2), F32)],
        compiler_params=_params("parallel", "parallel", "arbitrary"),
        name="moba",
    )(q, k, v, kaux, s256)


def _gla_kernel(qk_ref, v_ref, go_ref, gl_ref, wg_ref, bg_ref, gn_ref, o_ref, st_ref,
                *, sub, chunk):
    kw = N_GLA_HEADS * GLA_DK
    vw = N_GLA_HEADS * GLA_DV
    tt = qk_ref.shape[1]

    @pl.when(pl.program_id(1) == 0)
    def _reset():
        st_ref[...] = jnp.zeros_like(st_ref)

    r = lax.broadcasted_iota(jnp.int32, (sub, sub), 0)
    c = lax.broadcasted_iota(jnp.int32, (sub, sub), 1)
    same = (r >> 6) == (c >> 6)
    causal = same & (c <= r)
    ltri = jnp.where(causal, 1.0, 0.0).astype(BF16)
    eblk = jnp.where(same, 1.0, 0.0).astype(BF16)
    st_diag = ((lax.broadcasted_iota(jnp.int32, (vw, kw), 0) >> 6)
               == (lax.broadcasted_iota(jnp.int32, (vw, kw), 1) >> 5))
    grp = jnp.where((lax.broadcasted_iota(jnp.int32, (vw, vw), 0) >> 6)
                    == (lax.broadcasted_iota(jnp.int32, (vw, vw), 1) >> 6), 1.0, 0.0).astype(BF16)
    khead = lax.broadcasted_iota(jnp.int32, (sub, kw), 1) >> 5
    vhead = lax.broadcasted_iota(jnp.int32, (sub, vw), 1) >> 6

    for s in range(tt // sub):
        rows = slice(s * sub, (s + 1) * sub)
        z = _dot(gl_ref[0, rows, :], wg_ref[...]) + bg_ref[...]
        la = (jnp.minimum(z, 0.0) - jnp.log(1.0 + jnp.exp(-jnp.abs(z)))) * (1.0 / GLA_GATE_NORMALIZER)
        la_hi, la_lo = _split_bf16(la)
        b = _dot(ltri, la_hi) + _dot(ltri, la_lo)
        bl = _dot(eblk, la_hi) + _dot(eblk, la_lo)
        q = qk_ref[0, rows, 0:kw].astype(F32)
        k = qk_ref[0, rows, kw:2 * kw].astype(F32)
        v = v_ref[0, rows, :]
        qd = (q * (jnp.exp(b) * (GLA_DK ** -0.5))).astype(BF16)
        kd = (k * jnp.exp(-b)).astype(BF16)
        kend = (k * jnp.exp(bl - b)).astype(BF16)

        o = jnp.zeros((sub, vw), F32)
        for h in range(N_GLA_HEADS):
            a = _dot_nt(jnp.where(khead == h, qd, jnp.zeros_like(qd)), kd)
            a = jnp.where(causal, a, 0.0).astype(BF16)
            o = o + _dot(a, jnp.where(vhead == h, v, jnp.zeros_like(v)))

        inter = []
        for ci in range(sub // chunk):
            cr = slice(ci * chunk, (ci + 1) * chunk)
            st = st_ref[...]
            inter.append(_dot_nt(qd[cr], st.astype(BF16)))
            ut = _dot_tn(v[cr], kend[cr])
            decay = jnp.exp(bl[ci * chunk:ci * chunk + 1, :])
            st_ref[...] = st * decay + jnp.where(st_diag, ut, 0.0)
        o = o + jnp.concatenate(inter, axis=0)

        o2_hi, o2_lo = _split_bf16(o * o)
        ms = (_dot(o2_hi, grp) + _dot(o2_lo, grp)) * (1.0 / GLA_DV)
        go = go_ref[0, rows, :].astype(F32)
        y = (o * lax.rsqrt(ms + EPS)) * gn_ref[...] * (go * (1.0 / (1.0 + jnp.exp(-go))))
        o_ref[0, rows, :] = y.astype(o_ref.dtype)


def _gla(qk, v, go, gl, wg, bg, gn, tt):
    b, t, _ = qk.shape
    kw = N_GLA_HEADS * GLA_DK
    vw = N_GLA_HEADS * GLA_DV
    spec = lambda w: pl.BlockSpec((1, tt, w), lambda bi, i: (bi, i, 0))
    return pl.pallas_call(
        functools.partial(_gla_kernel, sub=4 * GLA_CHUNK, chunk=GLA_CHUNK),
        grid=(b, t // tt),
        in_specs=[spec(2 * kw), spec(vw), spec(vw), spec(LANES),
                  _resident(wg.shape), _resident(bg.shape), _resident(gn.shape)],
        out_specs=spec(vw),
        out_shape=jax.ShapeDtypeStruct((b, t, vw), BF16),
        scratch_shapes=[pltpu.VMEM((vw, kw), F32)],
        compiler_params=_params("parallel", "arbitrary"),
        name="gla",
    )(qk, v, go, gl, wg, bg, gn)


def _mem_kv_kernel(mem_ref, g_ref, w_ref, k_ref, v_ref):
    h = _rms(mem_ref[0], g_ref[0]).astype(BF16)
    kv = _dot(h, w_ref[0])
    wd = k_ref.shape[-1]
    k_ref[0, 0] = kv[:, :wd].astype(BF16)
    v_ref[0, 0] = kv[:, wd:].astype(BF16)


def _mem_kv(mem, g, wkv):
    b, m, d = mem.shape
    nl = wkv.shape[0]
    wd = wkv.shape[-1] // 2
    out = jax.ShapeDtypeStruct((nl, b, m, wd), BF16)
    ospec = pl.BlockSpec((1, 1, m, wd), lambda l, bi: (l, bi, 0, 0))
    return pl.pallas_call(
        _mem_kv_kernel,
        grid=(nl, b),
        in_specs=[pl.BlockSpec((1, m, d), lambda l, bi: (bi, 0, 0)),
                  pl.BlockSpec((1, 1, d), lambda l, bi: (l, 0, 0)),
                  pl.BlockSpec((1, d, 2 * wd), lambda l, bi: (l, 0, 0))],
        out_specs=[ospec, ospec],
        out_shape=[out, out],
        compiler_params=_params("parallel", "parallel"),
        name="mem_kv",
    )(mem, g, wkv)


def _mix_xattn_kernel(x_ref, c3_ref, c3h_ref, ym_ref, yg_ref, wout_ref, cw_ref, cb_ref,
                      gx_ref, wq_ref, mk_ref, mv_ref, wo_ref, o_ref, *, tiles_per_seq):
    tm = x_ref.shape[0]
    cc = cw_ref.shape[-1]
    nh = N_XATTN_HEADS
    first = (pl.program_id(0) % tiles_per_seq) == 0

    c3 = c3_ref[...].astype(F32)
    u = c3[:, cc:2 * cc] * c3[:, 2 * cc:3 * cc]
    halo = c3h_ref[...].astype(F32)
    uh = halo[:, cc:2 * cc] * halo[:, 2 * cc:3 * cc]
    uh = jnp.where(first, 0.0, uh)
    last1 = uh[BF16_SUBLANES - 1:BF16_SUBLANES, :]
    last2 = uh[BF16_SUBLANES - 2:BF16_SUBLANES - 1, :]
    row = lax.broadcasted_iota(jnp.int32, (tm, cc), 0)
    u1 = jnp.where(row == 0, last1, pltpu.roll(u, 1, 0))
    u2 = jnp.where(row == 0, last2, jnp.where(row == 1, last1, pltpu.roll(u, 2, 0)))
    cw = cw_ref[...]
    yc = c3[:, 0:cc] * (cw[0:1] * u2 + cw[1:2] * u1 + cw[2:3] * u + cb_ref[...])

    mw = ym_ref.shape[-1]
    x1 = (x_ref[...]
          + _dot(yc.astype(BF16), wout_ref[0:cc, :])
          + _dot(ym_ref[...], wout_ref[cc:cc + mw, :])
          + _dot(yg_ref[...], wout_ref[cc + mw:, :]))

    h = _rms(x1, gx_ref[...]).astype(BF16)
    q = _dot(h, wq_ref[...]).astype(BF16)
    k = mk_ref[0, 0]
    v = mv_ref[0, 0]
    qhead = lax.broadcasted_iota(jnp.int32, q.shape, 1) >> 6
    vhead = lax.broadcasted_iota(jnp.int32, v.shape, 1) >> 6
    o = jnp.zeros(q.shape, F32)
    for hd in range(nh):
        s = _dot_nt(jnp.where(qhead == hd, q, jnp.zeros_like(q)), k)
        p = jnp.exp(s - jnp.max(s, axis=-1, keepdims=True))
        l = jnp.sum(p, axis=-1, keepdims=True)
        pv = _dot(p.astype(BF16), jnp.where(vhead == hd, v, jnp.zeros_like(v)))
        o = o + pv * (1.0 / l)
    o_ref[...] = x1 + _dot(o.astype(BF16), wo_ref[...])


def _mix_xattn(x, c3, ym, yg, wout, cw, cb, gx, wq, memk_l, memv_l, wo, *, tm, seq):
    n, d = x.shape
    tiles_per_seq = seq // tm
    hb = tm // BF16_SUBLANES
    row = lambda w: pl.BlockSpec((tm, w), lambda i: (i, 0))
    mem_spec = pl.BlockSpec((1, 1) + memk_l.shape[2:], lambda i: (0, i // tiles_per_seq, 0, 0))
    return pl.pallas_call(
        functools.partial(_mix_xattn_kernel, tiles_per_seq=tiles_per_seq),
        grid=(n // tm,),
        in_specs=[row(d), row(c3.shape[1]),
                  pl.BlockSpec((BF16_SUBLANES, c3.shape[1]),
                               lambda i: (jnp.maximum(i * hb - 1, 0), 0)),
                  row(ym.shape[1]), row(yg.shape[1]),
                  _resident(wout.shape), _resident(cw.shape), _resident(cb.shape),
                  _resident(gx.shape), _resident(wq.shape),
                  mem_spec, mem_spec, _resident(wo.shape)],
        out_specs=row(d),
        out_shape=jax.ShapeDtypeStruct((n, d), F32),
        compiler_params=_params("parallel"),
        name="mix_xattn",
    )(x, c3, c3, ym, yg, wout, cw, cb, gx, wq, memk_l, memv_l, wo)


def _ffn_kernel(x_ref, g_ref, wa_ref, wg_ref, cwa_ref, cwg_ref, cba_ref, cbg_ref, wd_ref,
                gf_ref, o_ref, hext_ref, acc_ref, *, tiles_per_seq, final):
    tm = x_ref.shape[0]
    halo = BF16_SUBLANES
    first = (pl.program_id(0) % tiles_per_seq) == 0

    @pl.when(first)
    def _zero_halo():
        hext_ref[0:halo, :] = jnp.zeros((halo, hext_ref.shape[1]), BF16)

    @pl.when(jnp.logical_not(first))
    def _carry_halo():
        hext_ref[0:halo, :] = hext_ref[tm:tm + halo, :]

    x = x_ref[...]
    hext_ref[halo:, :] = _rms(x, g_ref[...]).astype(BF16)
    hext = hext_ref[...]
    acc_ref[...] = jnp.zeros_like(acc_ref)

    def conv(u, cw, cb):
        y = cw[0:1] * pltpu.roll(u, 2, 0) + cw[1:2] * pltpu.roll(u, 1, 0) + cw[2:3] * u + cb
        return y[halo:]

    def body(j, carry):
        a = conv(_dot(hext, wa_ref[j]), cwa_ref[j], cba_ref[j])
        g = conv(_dot(hext, wg_ref[j]), cwg_ref[j], cbg_ref[j])
        hid = (a * (1.0 / (1.0 + jnp.exp(-a)))) * g
        acc_ref[...] += _dot(hid.astype(BF16), wd_ref[j])
        return carry

    lax.fori_loop(0, wa_ref.shape[0], body, 0)
    y = x + acc_ref[...]
    if final:
        y = _rms(y, gf_ref[...])
    o_ref[...] = y


def _ffn(x, g, wa, wg, cwa, cwg, cba, cbg, wd, gf, *, tm, seq, final):
    n, d = x.shape
    return pl.pallas_call(
        functools.partial(_ffn_kernel, tiles_per_seq=seq // tm, final=final),
        grid=(n // tm,),
        in_specs=[pl.BlockSpec((tm, d), lambda i: (i, 0)), _resident(g.shape),
                  _resident(wa.shape), _resident(wg.shape),
                  _resident(cwa.shape), _resident(cwg.shape),
                  _resident(cba.shape), _resident(cbg.shape),
                  _resident(wd.shape), _resident(gf.shape)],
        out_specs=pl.BlockSpec((tm, d), lambda i: (i, 0)),
        out_shape=jax.ShapeDtypeStruct((n, d), F32),
        scratch_shapes=[pltpu.VMEM((tm + BF16_SUBLANES, d), BF16),
                        pltpu.VMEM((tm, d), F32)],
        compiler_params=_params("arbitrary"),
        name="ffn",
    )(x, g, wa, wg, cwa, cwg, cba, cbg, wd, gf)


def _chunk_cols(w, pad_to, tn):
    w = jnp.pad(w, [(0, 0)] * (w.ndim - 1) + [(0, pad_to - w.shape[-1])])
    w = w.reshape(w.shape[:-1] + (pad_to // tn, tn))
    return jnp.moveaxis(w, -2, -3)


def kernel(x, mem, norm_mix_g, w_in, sc_conv_w, sc_conv_b, gla_w_gate, gla_b_gate, gla_norm_g,
           w_out, norm_xattn_g, norm_mem_g, xattn_wq, xattn_wkv, xattn_wo, norm_ffn_g,
           ffn_w_up, ffn_conv_w, ffn_conv_b, ffn_w_down, final_norm_g):
    bsz, seq, d = x.shape
    nl = w_in.shape[0]
    n = bsz * seq
    conv_ch = sc_conv_w.shape[-1]
    moba_w = N_MOBA_HEADS * HEAD_DIM
    gk_w = N_GLA_HEADS * GLA_DK
    gv_w = N_GLA_HEADS * GLA_DV
    d_ff = ffn_w_down.shape[1]
    tm = 512
    ff_tn = 256
    ff_pad = -(-d_ff // ff_tn) * ff_tn

    widths = (3 * conv_ch, moba_w, moba_w, moba_w, 2 * gk_w, gv_w, gv_w, LANES)
    in_w = w_in.shape[-1]
    q0 = 3 * conv_ch
    col_scale = jnp.ones((in_w,), F32).at[q0:q0 + moba_w].set(HEAD_DIM ** -0.5)
    w_in_p = jnp.pad(w_in * col_scale, ((0, 0), (0, 0), (0, sum(widths) - in_w))).astype(BF16)
    wg_p = jnp.pad(gla_w_gate, ((0, 0), (0, LANES - GLA_GATE_RANK), (0, 0))).astype(BF16)
    gn_t = jnp.tile(gla_norm_g, (1, N_GLA_HEADS))
    w_out_b = w_out.astype(BF16)
    wq_b = (xattn_wq * (HEAD_DIM ** -0.5)).astype(BF16)
    wkv_b = xattn_wkv.astype(BF16)
    wo_b = xattn_wo.astype(BF16)
    wa = _chunk_cols(ffn_w_up[..., :d_ff], ff_pad, ff_tn).astype(BF16)
    wgt = _chunk_cols(ffn_w_up[..., d_ff:], ff_pad, ff_tn).astype(BF16)
    cwa = _chunk_cols(ffn_conv_w[..., :d_ff], ff_pad, ff_tn)
    cwg = _chunk_cols(ffn_conv_w[..., d_ff:], ff_pad, ff_tn)
    cba = _chunk_cols(ffn_conv_b[:, None, :d_ff], ff_pad, ff_tn)
    cbg = _chunk_cols(ffn_conv_b[:, None, d_ff:], ff_pad, ff_tn)
    wd = jnp.pad(ffn_w_down, ((0, 0), (0, ff_pad - d_ff), (0, 0))).astype(BF16)
    wd = wd.reshape(nl, ff_pad // ff_tn, ff_tn, d)

    memk, memv = _mem_kv(mem, norm_mem_g[:, None, :], wkv_b)

    xf = x.reshape(n, d)
    for l in range(nl):
        c3, mq, mk, mv, gqk, gv, go, gl = _norm_inproj(
            xf, norm_mix_g[l][None, :], w_in_p[l], widths, tm)
        r3 = lambda a: a.reshape(bsz, seq, a.shape[-1])
        ym = _moba(r3(mq), r3(mk), r3(mv)).reshape(n, moba_w)
        yg = _gla(r3(gqk), r3(gv), r3(go), r3(gl), wg_p[l], gla_b_gate[l][None, :],
                  gn_t[l][None, :], tt=1024).reshape(n, gv_w)
        xf = _mix_xattn(xf, c3, ym, yg, w_out_b[l], sc_conv_w[l], sc_conv_b[l][None, :],
                        norm_xattn_g[l][None, :], wq_b[l], memk[l:l + 1], memv[l:l + 1], wo_b[l],
                        tm=tm, seq=seq)
        xf = _ffn(xf, norm_ffn_g[l][None, :], wa[l], wgt[l], cwa[l], cwg[l], cba[l], cbg[l], wd[l],
                  final_norm_g[None, :], tm=tm, seq=seq, final=(l == nl - 1))
    return xf.reshape(bsz, seq, d)
```

```python
import functools

import numpy as np
import jax
import jax.numpy as jnp
from jax import lax
from jax.experimental import pallas as pl
from jax.experimental.pallas import tpu as pltpu

F32 = jnp.float32
BF16 = jnp.bfloat16

EPS = 1e-6
HEAD_DIM = 64
N_MOBA_HEADS = 8
MOBA_BLOCK = 256
MOBA_TOPK = 3
N_GLA_HEADS = 4
GLA_DK = 32
GLA_DV = 64
GLA_CHUNK = 64
GLA_GATE_RANK = 16
GLA_GATE_NORMALIZER = 16.0
CONV_K = 3
N_XATTN_HEADS = 4

LANES = 128
F32_SUBLANES = 8
BF16_SUBLANES = 16
VMEM_LIMIT = 56 * 1024 * 1024

VT_ROWS = HEAD_DIM + 16
MASK_BIAS = -(2.0 ** 60)

_NT = (((1,), (1,)), ((), ()))
_TN = (((0,), (0,)), ((), ()))


def _dot(a, b):
    return jnp.dot(a, b, preferred_element_type=F32)


def _dot_nt(a, b):
    return lax.dot_general(a, b, _NT, preferred_element_type=F32)


def _dot_tn(a, b):
    return lax.dot_general(a, b, _TN, preferred_element_type=F32)


def _split_bf16(a):
    hi = a.astype(BF16)
    lo = (a - hi.astype(F32)).astype(BF16)
    return hi, lo


def _rms(x, g):
    r = lax.rsqrt(jnp.mean(x * x, axis=-1, keepdims=True) + EPS)
    return (x * r) * g


def _params(*sem):
    return pltpu.CompilerParams(dimension_semantics=sem, vmem_limit_bytes=VMEM_LIMIT)


def _resident(shape):
    nd = len(shape)
    return pl.BlockSpec(shape, lambda *_: (0,) * nd, pipeline_mode=pl.Buffered(1))


def _norm_inproj_kernel(x_ref, g_ref, w_ref, *out_refs):
    h = _rms(x_ref[...], g_ref[...]).astype(BF16)
    off = 0
    for o_ref in out_refs:
        wd = o_ref.shape[-1]
        o_ref[...] = _dot(h, w_ref[:, off:off + wd]).astype(o_ref.dtype)
        off += wd


def _norm_inproj(x, g, w, widths, tm):
    n, d = x.shape
    return pl.pallas_call(
        _norm_inproj_kernel,
        grid=(n // tm,),
        in_specs=[pl.BlockSpec((tm, d), lambda i: (i, 0)),
                  _resident((1, d)),
                  _resident(w.shape)],
        out_specs=[pl.BlockSpec((tm, wd), lambda i: (i, 0)) for wd in widths],
        out_shape=[jax.ShapeDtypeStruct((n, wd), BF16) for wd in widths],
        compiler_params=_params("parallel"),
        name="norm_inproj",
    )(x, g, w)


def _moba_kernel(q_ref, k_ref, v_ref, kaux_ref, s256_ref, o_ref,
                 kaug_ref, vt_ref, kmean_ref, s_ref, p_ref, acc_ref, *, nblk, blk):
    j = pl.program_id(2)
    hd2 = 2 * HEAD_DIM
    tq = 2 * blk
    kt = 2 * blk

    @pl.when(j == 0)
    def _prepare():
        lane = lax.broadcasted_iota(jnp.int32, (blk, hd2), 1)
        aux = kaux_ref[0].astype(F32)
        ones = jnp.ones((VT_ROWS - HEAD_DIM, blk), BF16)
        for n in range(nblk):
            rows = slice(n * blk, (n + 1) * blk)
            kb = k_ref[0, rows, :]
            kaug_ref[rows, 0:hd2] = kb
            kaug_ref[rows, hd2:2 * hd2] = (aux + jnp.where(lane == n, 1.0, 0.0)).astype(BF16)
            vt = v_ref[0, rows, :].astype(F32).T.astype(BF16)
            for hh in range(2):
                vt_ref[hh, 0:HEAD_DIM, rows] = vt[hh * HEAD_DIM:(hh + 1) * HEAD_DIM]
                vt_ref[hh, HEAD_DIM:VT_ROWS, rows] = ones
            kmean_ref[n:n + 1, :] = jnp.sum(kb.astype(F32), axis=0, keepdims=True) * (1.0 / blk)

    qt = q_ref[0].astype(F32).T
    qt_b = qt.astype(BF16)
    km = kmean_ref[...]
    km_lane_head = lax.broadcasted_iota(jnp.int32, (nblk, hd2), 1) >> 6
    qt_row_head = lax.broadcasted_iota(jnp.int32, (hd2, tq), 0) >> 6
    nidx = lax.broadcasted_iota(jnp.int32, (nblk, tq), 0)
    qblk = 2 * j + (lax.broadcasted_iota(jnp.int32, (nblk, tq), 1) >> 8)
    aux_row = lax.broadcasted_iota(jnp.int32, (hd2 - nblk, tq), 0)
    past = nidx < qblk

    qaug = []
    for hh in range(2):
        kmh = jnp.where(km_lane_head == hh, km, 0.0)
        kh, kl = _split_bf16(kmh)
        gate = _dot(kh, qt_b) + _dot(kl, qt_b)
        g = jnp.where(past, gate, -jnp.inf)
        rank = jnp.zeros((nblk, tq), F32)
        for m in range(nblk):
            gm = g[m:m + 1, :]
            rank = rank + jnp.where(m < nidx, jnp.where(gm >= g, 1.0, 0.0), jnp.where(gm > g, 1.0, 0.0))
        sel = (past & (rank < MOBA_TOPK)) | (nidx == qblk)
        s256 = s256_ref[0, hh:hh + 1, :]
        bias = jnp.where(sel, (nidx - qblk).astype(F32) * s256, MASK_BIAS)
        head_one = jnp.where(aux_row == hh, 1.0, 0.0)
        qaug.append(jnp.concatenate(
            [jnp.where(qt_row_head == hh, qt, 0.0), bias, head_one], axis=0).astype(BF16))
    qall = jnp.concatenate(qaug, axis=1)

    ncg = 4
    cg = [slice(c * blk, (c + 1) * blk) for c in range(ncg)]

    def softmax_c(st, m, c):
        mn = jnp.maximum(m, jnp.max(st, axis=0, keepdims=True))
        p_ref[:, cg[c]] = jnp.exp(st - mn).astype(BF16)
        return mn, jnp.exp(m - mn)

    def accumulate_c(n, alpha, c):
        off = pl.multiple_of(n * kt, kt)
        hh, a = c // 2, c % 2
        acc_ref[hh, :, a * blk:(a + 1) * blk] = (
            alpha * acc_ref[hh, :, a * blk:(a + 1) * blk]
            + _dot(vt_ref[hh, :, pl.ds(off, kt)], p_ref[:, cg[c]]))

    s_ref[...] = _dot(kaug_ref[0:kt, :], qall)
    p_ref[...] = jnp.zeros_like(p_ref)
    acc_ref[...] = jnp.zeros_like(acc_ref)
    m0 = tuple(jnp.full((1, blk), -jnp.inf, F32) for _ in range(ncg))
    a0 = tuple(jnp.ones((1, blk), F32) for _ in range(ncg))

    def body(i, carry):
        ms, alphas = carry
        off = pl.multiple_of((i + 1) * kt, kt)
        ktile = kaug_ref[pl.ds(off, kt), :]
        new_m, new_a, nxt = [], [], []
        for c in range(ncg):
            nxt.append(_dot(ktile, qall[:, cg[c]]))
            accumulate_c(jnp.maximum(i - 1, 0), alphas[c], c)
            mn, al = softmax_c(s_ref[:, cg[c]], ms[c], c)
            new_m.append(mn)
            new_a.append(al)
        for c in range(ncg):
            s_ref[:, cg[c]] = nxt[c]
        return tuple(new_m), tuple(new_a)

    ms, alphas = lax.fori_loop(0, j, body, (m0, a0))
    for c in range(ncg):
        accumulate_c(jnp.maximum(j - 1, 0), alphas[c], c)
    key_i = lax.broadcasted_iota(jnp.int32, (blk, blk), 0)
    qry_i = lax.broadcasted_iota(jnp.int32, (blk, blk), 1)
    causal = key_i <= qry_i
    off = pl.multiple_of(j * kt, kt)
    for c in range(ncg):
        hh, a = c // 2, c % 2
        parts = [jnp.where(causal, s_ref[0:blk, cg[c]], -jnp.inf)] if a == 0 else [
            s_ref[0:blk, cg[c]], jnp.where(causal, s_ref[blk:kt, cg[c]], -jnp.inf)]
        mn = ms[c]
        for st in parts:
            mn = jnp.maximum(mn, jnp.max(st, axis=0, keepdims=True))
        for r, st in enumerate(parts):
            p_ref[r * blk:(r + 1) * blk, cg[c]] = jnp.exp(st - mn).astype(BF16)
        nk = len(parts) * blk
        acc_ref[hh, :, a * blk:(a + 1) * blk] = (
            jnp.exp(ms[c] - mn) * acc_ref[hh, :, a * blk:(a + 1) * blk]
            + _dot(vt_ref[hh, :, pl.ds(off, nk)], p_ref[0:nk, cg[c]]))

    for a in range(2):
        ot = []
        for hh in range(2):
            acc = acc_ref[hh, :, a * blk:(a + 1) * blk]
            ot.append(acc[0:HEAD_DIM] * (1.0 / acc[HEAD_DIM:HEAD_DIM + 1]))
        o_ref[0, a * blk:(a + 1) * blk, :] = jnp.concatenate(ot, axis=0).T.astype(o_ref.dtype)


def _moba_tables(nblk):
    slopes = 2.0 ** (-8.0 * np.arange(1, N_MOBA_HEADS + 1) / N_MOBA_HEADS)
    npair = N_MOBA_HEADS // 2
    kaux = np.zeros((npair, MOBA_BLOCK, 2 * HEAD_DIM), np.float32)
    s256 = np.zeros((npair, 8, 2 * MOBA_BLOCK), np.float32)
    for p in range(npair):
        for hh in range(2):
            kaux[p, :, nblk + hh] = slopes[2 * p + hh] * np.arange(MOBA_BLOCK)
            s256[p, hh, :] = slopes[2 * p + hh] * MOBA_BLOCK
    return jnp.asarray(kaux, BF16), jnp.asarray(s256, F32)


def _moba(q, k, v):
    b, t, w = q.shape
    blk = MOBA_BLOCK
    nblk = t // blk
    assert blk == 256 and t % (2 * blk) == 0 and nblk % 8 == 0 and nblk <= HEAD_DIM
    hd2 = 2 * HEAD_DIM
    npair = w // hd2
    tq = 2 * blk
    kaux, s256 = _moba_tables(nblk)
    kv_spec = pl.BlockSpec((1, t, hd2), lambda bi, p, i: (bi, 0, p))
    return pl.pallas_call(
        functools.partial(_moba_kernel, nblk=nblk, blk=blk),
        grid=(b, npair, t // tq),
        in_specs=[pl.BlockSpec((1, tq, hd2), lambda bi, p, i: (bi, i, p)),
                  kv_spec, kv_spec,
                  pl.BlockSpec((1, blk, hd2), lambda bi, p, i: (p, 0, 0)),
                  pl.BlockSpec((1, 8, tq), lambda bi, p, i: (p, 0, 0))],
        out_specs=pl.BlockSpec((1, tq, hd2), lambda bi, p, i: (bi, i, p)),
        out_shape=jax.ShapeDtypeStruct((b, t, w), BF16),
        scratch_shapes=[pltpu.VMEM((t, 2 * hd2), BF16),
                        pltpu.VMEM((2, VT_ROWS, t), BF16),
                        pltpu.VMEM((nblk, hd2), F32),
                        pltpu.VMEM((2 * blk, 2 * tq), F32),
                        pltpu.VMEM((2 * blk, 2 * tq), BF16),
                        pltpu.VMEM((2, VT_ROWS, tq), F32)],
        compiler_params=_params("parallel", "parallel", "arbitrary"),
        name="moba",
    )(q, k, v, kaux, s256)


def _gla_kernel(qk_ref, v_ref, go_ref, gl_ref, wg_ref, bg_ref, gn_ref, o_ref, st_ref,
                *, sub, chunk):
    kw = N_GLA_HEADS * GLA_DK
    vw = N_GLA_HEADS * GLA_DV
    tt = qk_ref.shape[1]

    @pl.when(pl.program_id(1) == 0)
    def _reset():
        st_ref[...] = jnp.zeros_like(st_ref)

    r = lax.broadcasted_iota(jnp.int32, (sub, sub), 0)
    c = lax.broadcasted_iota(jnp.int32, (sub, sub), 1)
    same = (r >> 6) == (c >> 6)
    causal = same & (c <= r)
    ltri = jnp.where(causal, 1.0, 0.0).astype(BF16)
    eblk = jnp.where(same, 1.0, 0.0).astype(BF16)
    st_diag = ((lax.broadcasted_iota(jnp.int32, (vw, kw), 0) >> 6)
               == (lax.broadcasted_iota(jnp.int32, (vw, kw), 1) >> 5))
    grp = jnp.where((lax.broadcasted_iota(jnp.int32, (vw, vw), 0) >> 6)
                    == (lax.broadcasted_iota(jnp.int32, (vw, vw), 1) >> 6), 1.0, 0.0).astype(BF16)
    khead = lax.broadcasted_iota(jnp.int32, (sub, kw), 1) >> 5
    vhead = lax.broadcasted_iota(jnp.int32, (sub, vw), 1) >> 6

    for s in range(tt // sub):
        rows = slice(s * sub, (s + 1) * sub)
        z = _dot(gl_ref[0, rows, :], wg_ref[...]) + bg_ref[...]
        la = (jnp.minimum(z, 0.0) - jnp.log(1.0 + jnp.exp(-jnp.abs(z)))) * (1.0 / GLA_GATE_NORMALIZER)
        la_hi, la_lo = _split_bf16(la)
        b = _dot(ltri, la_hi) + _dot(ltri, la_lo)
        bl = _dot(eblk, la_hi) + _dot(eblk, la_lo)
        q = qk_ref[0, rows, 0:kw].astype(F32)
        k = qk_ref[0, rows, kw:2 * kw].astype(F32)
        v = v_ref[0, rows, :]
        qd = (q * (jnp.exp(b) * (GLA_DK ** -0.5))).astype(BF16)
        kd = (k * jnp.exp(-b)).astype(BF16)
        kend = (k * jnp.exp(bl - b)).astype(BF16)

        o = jnp.zeros((sub, vw), F32)
        for h in range(N_GLA_HEADS):
            a = _dot_nt(jnp.where(khead == h, qd, jnp.zeros_like(qd)), kd)
            a = jnp.where(causal, a, 0.0).astype(BF16)
            o = o + _dot(a, jnp.where(vhead == h, v, jnp.zeros_like(v)))

        inter = []
        for ci in range(sub // chunk):
            cr = slice(ci * chunk, (ci + 1) * chunk)
            st = st_ref[...]
            inter.append(_dot_nt(qd[cr], st.astype(BF16)))
            ut = _dot_tn(v[cr], kend[cr])
            decay = jnp.exp(bl[ci * chunk:ci * chunk + 1, :])
            st_ref[...] = st * decay + jnp.where(st_diag, ut, 0.0)
        o = o + jnp.concatenate(inter, axis=0)

        o2_hi, o2_lo = _split_bf16(o * o)
        ms = (_dot(o2_hi, grp) + _dot(o2_lo, grp)) * (1.0 / GLA_DV)
        go = go_ref[0, rows, :].astype(F32)
        y = (o * lax.rsqrt(ms + EPS)) * gn_ref[...] * (go * (1.0 / (1.0 + jnp.exp(-go))))
        o_ref[0, rows, :] = y.astype(o_ref.dtype)


def _gla(qk, v, go, gl, wg, bg, gn, tt):
    b, t, _ = qk.shape
    kw = N_GLA_HEADS * GLA_DK
    vw = N_GLA_HEADS * GLA_DV
    spec = lambda w: pl.BlockSpec((1, tt, w), lambda bi, i: (bi, i, 0))
    return pl.pallas_call(
        functools.partial(_gla_kernel, sub=4 * GLA_CHUNK, chunk=GLA_CHUNK),
        grid=(b, t // tt),
        in_specs=[spec(2 * kw), spec(vw), spec(vw), spec(LANES),
                  _resident(wg.shape), _resident(bg.shape), _resident(gn.shape)],
        out_specs=spec(vw),
        out_shape=jax.ShapeDtypeStruct((b, t, vw), BF16),
        scratch_shapes=[pltpu.VMEM((vw, kw), F32)],
        compiler_params=_params("parallel", "arbitrary"),
        name="gla",
    )(qk, v, go, gl, wg, bg, gn)


def _mem_kv_kernel(mem_ref, g_ref, w_ref, k_ref, v_ref):
    h = _rms(mem_ref[0], g_ref[0]).astype(BF16)
    kv = _dot(h, w_ref[0])
    wd = k_ref.shape[-1]
    k_ref[0, 0] = kv[:, :wd].astype(BF16)
    v_ref[0, 0] = kv[:, wd:].astype(BF16)


def _mem_kv(mem, g, wkv):
    b, m, d = mem.shape
    nl = wkv.shape[0]
    wd = wkv.shape[-1] // 2
    out = jax.ShapeDtypeStruct((nl, b, m, wd), BF16)
    ospec = pl.BlockSpec((1, 1, m, wd), lambda l, bi: (l, bi, 0, 0))
    return pl.pallas_call(
        _mem_kv_kernel,
        grid=(nl, b),
        in_specs=[pl.BlockSpec((1, m, d), lambda l, bi: (bi, 0, 0)),
                  pl.BlockSpec((1, 1, d), lambda l, bi: (l, 0, 0)),
                  pl.BlockSpec((1, d, 2 * wd), lambda l, bi: (l, 0, 0))],
        out_specs=[ospec, ospec],
        out_shape=[out, out],
        compiler_params=_params("parallel", "parallel"),
        name="mem_kv",
    )(mem, g, wkv)


def _mix_xattn_kernel(x_ref, c3_ref, c3h_ref, ym_ref, yg_ref, wout_ref, cw_ref, cb_ref,
                      gx_ref, wq_ref, mk_ref, mv_ref, wo_ref, o_ref, *, tiles_per_seq):
    tm = x_ref.shape[0]
    cc = cw_ref.shape[-1]
    nh = N_XATTN_HEADS
    first = (pl.program_id(0) % tiles_per_seq) == 0

    c3 = c3_ref[...].astype(F32)
    u = c3[:, cc:2 * cc] * c3[:, 2 * cc:3 * cc]
    halo = c3h_ref[...].astype(F32)
    uh = halo[:, cc:2 * cc] * halo[:, 2 * cc:3 * cc]
    uh = jnp.where(first, 0.0, uh)
    last1 = uh[BF16_SUBLANES - 1:BF16_SUBLANES, :]
    last2 = uh[BF16_SUBLANES - 2:BF16_SUBLANES - 1, :]
    row = lax.broadcasted_iota(jnp.int32, (tm, cc), 0)
    u1 = jnp.where(row == 0, last1, pltpu.roll(u, 1, 0))
    u2 = jnp.where(row == 0, last2, jnp.where(row == 1, last1, pltpu.roll(u, 2, 0)))
    cw = cw_ref[...]
    yc = c3[:, 0:cc] * (cw[0:1] * u2 + cw[1:2] * u1 + cw[2:3] * u + cb_ref[...])

    mw = ym_ref.shape[-1]
    x1 = (x_ref[...]
          + _dot(yc.astype(BF16), wout_ref[0:cc, :])
          + _dot(ym_ref[...], wout_ref[cc:cc + mw, :])
          + _dot(yg_ref[...], wout_ref[cc + mw:, :]))

    h = _rms(x1, gx_ref[...]).astype(BF16)
    q = _dot(h, wq_ref[...]).astype(BF16)
    k = mk_ref[0, 0]
    v = mv_ref[0, 0]
    qhead = lax.broadcasted_iota(jnp.int32, q.shape, 1) >> 6
    vhead = lax.broadcasted_iota(jnp.int32, v.shape, 1) >> 6
    o = jnp.zeros(q.shape, F32)
    for hd in range(nh):
        s = _dot_nt(jnp.where(qhead == hd, q, jnp.zeros_like(q)), k)
        p = jnp.exp(s - jnp.max(s, axis=-1, keepdims=True))
        l = jnp.sum(p, axis=-1, keepdims=True)
        pv = _dot(p.astype(BF16), jnp.where(vhead == hd, v, jnp.zeros_like(v)))
        o = o + pv * (1.0 / l)
    o_ref[...] = x1 + _dot(o.astype(BF16), wo_ref[...])


def _mix_xattn(x, c3, ym, yg, wout, cw, cb, gx, wq, memk_l, memv_l, wo, *, tm, seq):
    n, d = x.shape
    tiles_per_seq = seq // tm
    hb = tm // BF16_SUBLANES
    row = lambda w: pl.BlockSpec((tm, w), lambda i: (i, 0))
    mem_spec = pl.BlockSpec((1, 1) + memk_l.shape[2:], lambda i: (0, i // tiles_per_seq, 0, 0))
    return pl.pallas_call(
        functools.partial(_mix_xattn_kernel, tiles_per_seq=tiles_per_seq),
        grid=(n // tm,),
        in_specs=[row(d), row(c3.shape[1]),
                  pl.BlockSpec((BF16_SUBLANES, c3.shape[1]),
                               lambda i: (jnp.maximum(i * hb - 1, 0), 0)),
                  row(ym.shape[1]), row(yg.shape[1]),
                  _resident(wout.shape), _resident(cw.shape), _resident(cb.shape),
                  _resident(gx.shape), _resident(wq.shape),
                  mem_spec, mem_spec, _resident(wo.shape)],
        out_specs=row(d),
        out_shape=jax.ShapeDtypeStruct((n, d), F32),
        compiler_params=_params("parallel"),
        name="mix_xattn",
    )(x, c3, c3, ym, yg, wout, cw, cb, gx, wq, memk_l, memv_l, wo)


def _ffn_kernel(x_ref, g_ref, wa_ref, wg_ref, cwa_ref, cwg_ref, cba_ref, cbg_ref, wd_ref,
                gf_ref, o_ref, hext_ref, acc_ref, ua_ref, ug_ref, *, tiles_per_seq, final, nsub):
    tm = x_ref.shape[0]
    halo = BF16_SUBLANES
    nc = wa_ref.shape[0]
    rs = tm // nsub
    first = (pl.program_id(0) % tiles_per_seq) == 0

    @pl.when(first)
    def _zero_halo():
        hext_ref[0:halo, :] = jnp.zeros((halo, hext_ref.shape[1]), BF16)

    @pl.when(jnp.logical_not(first))
    def _carry_halo():
        hext_ref[0:halo, :] = hext_ref[tm:tm + halo, :]

    x = x_ref[...]
    hext_ref[halo:, :] = _rms(x, g_ref[...]).astype(BF16)
    acc_ref[...] = jnp.zeros_like(acc_ref)

    def up_rows(j, lo, hi):
        hrows = hext_ref[lo:hi, :]
        return _dot(hrows, wa_ref[j]), _dot(hrows, wg_ref[j])

    def conv(u, cw, cb):
        y = cw[0:1] * pltpu.roll(u, 2, 0) + cw[1:2] * pltpu.roll(u, 1, 0) + cw[2:3] * u + cb
        return y[F32_SUBLANES:]

    def down_rows(j, r):
        src = slice(halo + r * rs - F32_SUBLANES, halo + (r + 1) * rs)
        a = conv(ua_ref[src, :], cwa_ref[j], cba_ref[j])
        g = conv(ug_ref[src, :], cwg_ref[j], cbg_ref[j])
        hid = ((a * (1.0 / (1.0 + jnp.exp(-a)))) * g).astype(BF16)
        acc_ref[r * rs:(r + 1) * rs, :] += _dot(hid, wd_ref[j])

    bounds = [(0 if r == 0 else halo + r * rs, halo + (r + 1) * rs) for r in range(nsub)]
    ua_ref[...], ug_ref[...] = up_rows(0, 0, tm + halo)

    def body(j, carry):
        nxt = []
        for r, (lo, hi) in enumerate(bounds):
            nxt.append(up_rows(j + 1, lo, hi))
            down_rows(j, r)
        for (lo, hi), (ua, ug) in zip(bounds, nxt):
            ua_ref[lo:hi, :] = ua
            ug_ref[lo:hi, :] = ug
        return carry

    lax.fori_loop(0, nc - 1, body, 0)
    for r in range(nsub):
        down_rows(nc - 1, r)
    y = x + acc_ref[...]
    if final:
        y = _rms(y, gf_ref[...])
    o_ref[...] = y


def _ffn(x, g, wa, wg, cwa, cwg, cba, cbg, wd, gf, *, tm, seq, final):
    n, d = x.shape
    tn = wa.shape[-1]
    return pl.pallas_call(
        functools.partial(_ffn_kernel, tiles_per_seq=seq // tm, final=final, nsub=tm // 128),
        grid=(n // tm,),
        in_specs=[pl.BlockSpec((tm, d), lambda i: (i, 0)), _resident(g.shape),
                  _resident(wa.shape), _resident(wg.shape),
                  _resident(cwa.shape), _resident(cwg.shape),
                  _resident(cba.shape), _resident(cbg.shape),
                  _resident(wd.shape), _resident(gf.shape)],
        out_specs=pl.BlockSpec((tm, d), lambda i: (i, 0)),
        out_shape=jax.ShapeDtypeStruct((n, d), F32),
        scratch_shapes=[pltpu.VMEM((tm + BF16_SUBLANES, d), BF16),
                        pltpu.VMEM((tm, d), F32),
                        pltpu.VMEM((tm + BF16_SUBLANES, tn), F32),
                        pltpu.VMEM((tm + BF16_SUBLANES, tn), F32)],
        compiler_params=_params("arbitrary"),
        name="ffn",
    )(x, g, wa, wg, cwa, cwg, cba, cbg, wd, gf)


def _chunk_cols(w, pad_to, tn):
    w = jnp.pad(w, [(0, 0)] * (w.ndim - 1) + [(0, pad_to - w.shape[-1])])
    w = w.reshape(w.shape[:-1] + (pad_to // tn, tn))
    return jnp.moveaxis(w, -2, -3)


def kernel(x, mem, norm_mix_g, w_in, sc_conv_w, sc_conv_b, gla_w_gate, gla_b_gate, gla_norm_g,
           w_out, norm_xattn_g, norm_mem_g, xattn_wq, xattn_wkv, xattn_wo, norm_ffn_g,
           ffn_w_up, ffn_conv_w, ffn_conv_b, ffn_w_down, final_norm_g):
    bsz, seq, d = x.shape
    nl = w_in.shape[0]
    n = bsz * seq
    conv_ch = sc_conv_w.shape[-1]
    moba_w = N_MOBA_HEADS * HEAD_DIM
    gk_w = N_GLA_HEADS * GLA_DK
    gv_w = N_GLA_HEADS * GLA_DV
    d_ff = ffn_w_down.shape[1]
    tm = 512
    ff_tn = 256
    ff_pad = -(-d_ff // ff_tn) * ff_tn

    widths = (3 * conv_ch, moba_w, moba_w, moba_w, 2 * gk_w, gv_w, gv_w, LANES)
    in_w = w_in.shape[-1]
    q0 = 3 * conv_ch
    col_scale = jnp.ones((in_w,), F32).at[q0:q0 + moba_w].set(HEAD_DIM ** -0.5)
    w_in_p = jnp.pad(w_in * col_scale, ((0, 0), (0, 0), (0, sum(widths) - in_w))).astype(BF16)
    wg_p = jnp.pad(gla_w_gate, ((0, 0), (0, LANES - GLA_GATE_RANK), (0, 0))).astype(BF16)
    gn_t = jnp.tile(gla_norm_g, (1, N_GLA_HEADS))
    w_out_b = w_out.astype(BF16)
    wq_b = (xattn_wq * (HEAD_DIM ** -0.5)).astype(BF16)
    wkv_b = xattn_wkv.astype(BF16)
    wo_b = xattn_wo.astype(BF16)
    wa = _chunk_cols(ffn_w_up[..., :d_ff], ff_pad, ff_tn).astype(BF16)
    wgt = _chunk_cols(ffn_w_up[..., d_ff:], ff_pad, ff_tn).astype(BF16)
    cwa = _chunk_cols(ffn_conv_w[..., :d_ff], ff_pad, ff_tn)
    cwg = _chunk_cols(ffn_conv_w[..., d_ff:], ff_pad, ff_tn)
    cba = _chunk_cols(ffn_conv_b[:, None, :d_ff], ff_pad, ff_tn)
    cbg = _chunk_cols(ffn_conv_b[:, None, d_ff:], ff_pad, ff_tn)
    wd = jnp.pad(ffn_w_down, ((0, 0), (0, ff_pad - d_ff), (0, 0))).astype(BF16)
    wd = wd.reshape(nl, ff_pad // ff_tn, ff_tn, d)

    memk, memv = _mem_kv(mem, norm_mem_g[:, None, :], wkv_b)

    xf = x.reshape(n, d)
    for l in range(nl):
        c3, mq, mk, mv, gqk, gv, go, gl = _norm_inproj(
            xf, norm_mix_g[l][None, :], w_in_p[l], widths, tm)
        r3 = lambda a: a.reshape(bsz, seq, a.shape[-1])
        ym = _moba(r3(mq), r3(mk), r3(mv)).reshape(n, moba_w)
        yg = _gla(r3(gqk), r3(gv), r3(go), r3(gl), wg_p[l], gla_b_gate[l][None, :],
                  gn_t[l][None, :], tt=1024).reshape(n, gv_w)
        xf = _mix_xattn(xf, c3, ym, yg, w_out_b[l], sc_conv_w[l], sc_conv_b[l][None, :],
                        norm_xattn_g[l][None, :], wq_b[l], memk[l:l + 1], memv[l:l + 1], wo_b[l],
                        tm=tm, seq=seq)
        xf = _ffn(xf, norm_ffn_g[l][None, :], wa[l], wgt[l], cwa[l], cwg[l], cba[l], cbg[l], wd[l],
                  final_norm_g[None, :], tm=tm, seq=seq, final=(l == nl - 1))
    return xf.reshape(bsz, seq, d)
```

```python
import functools

import numpy as np
import jax
import jax.numpy as jnp
from jax import lax
from jax.experimental import pallas as pl
from jax.experimental.pallas import tpu as pltpu

F32 = jnp.float32
BF16 = jnp.bfloat16

EPS = 1e-6
HEAD_DIM = 64
N_MOBA_HEADS = 8
MOBA_BLOCK = 256
MOBA_TOPK = 3
N_GLA_HEADS = 4
GLA_DK = 32
GLA_DV = 64
GLA_CHUNK = 64
GLA_GATE_RANK = 16
GLA_GATE_NORMALIZER = 16.0
CONV_K = 3
N_XATTN_HEADS = 4

LANES = 128
F32_SUBLANES = 8
BF16_SUBLANES = 16
VMEM_LIMIT = 56 * 1024 * 1024
FFN_LAYER_CONFIG = ((1024, 2), (1024, 1), (512, 1), (512, 2))

VT_ROWS = HEAD_DIM + 16
MASK_BIAS = -(2.0 ** 60)

_NT = (((1,), (1,)), ((), ()))
_TN = (((0,), (0,)), ((), ()))


def _dot(a, b):
    return jnp.dot(a, b, preferred_element_type=F32)


def _dot_nt(a, b):
    return lax.dot_general(a, b, _NT, preferred_element_type=F32)


def _dot_tn(a, b):
    return lax.dot_general(a, b, _TN, preferred_element_type=F32)


def _split_bf16(a):
    hi = a.astype(BF16)
    lo = (a - hi.astype(F32)).astype(BF16)
    return hi, lo


def _rms(x, g):
    r = lax.rsqrt(jnp.mean(x * x, axis=-1, keepdims=True) + EPS)
    return (x * r) * g


def _params(*sem):
    return pltpu.CompilerParams(dimension_semantics=sem, vmem_limit_bytes=VMEM_LIMIT)


def _resident(shape):
    nd = len(shape)
    return pl.BlockSpec(shape, lambda *_: (0,) * nd, pipeline_mode=pl.Buffered(1))


def _norm_inproj_kernel(x_ref, g_ref, w_ref, *out_refs):
    h = _rms(x_ref[...], g_ref[...]).astype(BF16)
    off = 0
    for o_ref in out_refs:
        wd = o_ref.shape[-1]
        o_ref[...] = _dot(h, w_ref[:, off:off + wd]).astype(o_ref.dtype)
        off += wd


def _norm_inproj(x, g, w, widths, tm):
    n, d = x.shape
    return pl.pallas_call(
        _norm_inproj_kernel,
        grid=(n // tm,),
        in_specs=[pl.BlockSpec((tm, d), lambda i: (i, 0)),
                  _resident((1, d)),
                  _resident(w.shape)],
        out_specs=[pl.BlockSpec((tm, wd), lambda i: (i, 0)) for wd in widths],
        out_shape=[jax.ShapeDtypeStruct((n, wd), BF16) for wd in widths],
        compiler_params=_params("parallel"),
        name="norm_inproj",
    )(x, g, w)


def _moba_kernel(q_ref, k_ref, v_ref, kaux_ref, s256_ref, o_ref,
                 kaug_ref, vt_ref, kmean_ref, s_ref, p_ref, acc_ref, *, nblk, blk):
    j = pl.program_id(2)
    hd2 = 2 * HEAD_DIM
    tq = 2 * blk
    kt = 2 * blk

    @pl.when(j == 0)
    def _prepare():
        lane = lax.broadcasted_iota(jnp.int32, (blk, hd2), 1)
        aux = kaux_ref[0].astype(F32)
        ones = jnp.ones((VT_ROWS - HEAD_DIM, blk), BF16)
        for n in range(nblk):
            rows = slice(n * blk, (n + 1) * blk)
            kb = k_ref[0, rows, :]
            kaug_ref[rows, 0:hd2] = kb
            kaug_ref[rows, hd2:2 * hd2] = (aux + jnp.where(lane == n, 1.0, 0.0)).astype(BF16)
            vt = v_ref[0, rows, :].astype(F32).T.astype(BF16)
            for hh in range(2):
                vt_ref[hh, 0:HEAD_DIM, rows] = vt[hh * HEAD_DIM:(hh + 1) * HEAD_DIM]
                vt_ref[hh, HEAD_DIM:VT_ROWS, rows] = ones
            kmean_ref[n:n + 1, :] = jnp.sum(kb.astype(F32), axis=0, keepdims=True) * (1.0 / blk)

    qt = q_ref[0].astype(F32).T
    qt_b = qt.astype(BF16)
    km = kmean_ref[...]
    km_lane_head = lax.broadcasted_iota(jnp.int32, (nblk, hd2), 1) >> 6
    qt_row_head = lax.broadcasted_iota(jnp.int32, (hd2, tq), 0) >> 6
    nidx = lax.broadcasted_iota(jnp.int32, (nblk, tq), 0)
    qblk = 2 * j + (lax.broadcasted_iota(jnp.int32, (nblk, tq), 1) >> 8)
    aux_row = lax.broadcasted_iota(jnp.int32, (hd2 - nblk, tq), 0)
    past = nidx < qblk

    qaug = []
    for hh in range(2):
        kmh = jnp.where(km_lane_head == hh, km, 0.0)
        kh, kl = _split_bf16(kmh)
        gate = _dot(kh, qt_b) + _dot(kl, qt_b)
        g = jnp.where(past, gate, -jnp.inf)
        rank = jnp.zeros((nblk, tq), F32)
        for m in range(nblk):
            gm = g[m:m + 1, :]
            rank = rank + jnp.where(m < nidx, jnp.where(gm >= g, 1.0, 0.0), jnp.where(gm > g, 1.0, 0.0))
        sel = (past & (rank < MOBA_TOPK)) | (nidx == qblk)
        s256 = s256_ref[0, hh:hh + 1, :]
        bias = jnp.where(sel, (nidx - qblk).astype(F32) * s256, MASK_BIAS)
        head_one = jnp.where(aux_row == hh, 1.0, 0.0)
        qaug.append(jnp.concatenate(
            [jnp.where(qt_row_head == hh, qt, 0.0), bias, head_one], axis=0).astype(BF16))
    qall = jnp.concatenate(qaug, axis=1)

    ncg = 4
    cg = [slice(c * blk, (c + 1) * blk) for c in range(ncg)]

    def softmax_c(st, m, c):
        mn = jnp.maximum(m, jnp.max(st, axis=0, keepdims=True))
        p_ref[:, cg[c]] = jnp.exp(st - mn).astype(BF16)
        return mn, jnp.exp(m - mn)

    def accumulate_c(n, alpha, c):
        off = pl.multiple_of(n * kt, kt)
        hh, a = c // 2, c % 2
        acc_ref[hh, :, a * blk:(a + 1) * blk] = (
            alpha * acc_ref[hh, :, a * blk:(a + 1) * blk]
            + _dot(vt_ref[hh, :, pl.ds(off, kt)], p_ref[:, cg[c]]))

    s_ref[...] = _dot(kaug_ref[0:kt, :], qall)
    p_ref[...] = jnp.zeros_like(p_ref)
    acc_ref[...] = jnp.zeros_like(acc_ref)
    m0 = tuple(jnp.full((1, blk), -jnp.inf, F32) for _ in range(ncg))
    a0 = tuple(jnp.ones((1, blk), F32) for _ in range(ncg))

    def body(i, carry):
        ms, alphas = carry
        off = pl.multiple_of((i + 1) * kt, kt)
        ktile = kaug_ref[pl.ds(off, kt), :]
        new_m, new_a, nxt = [], [], []
        for c in range(ncg):
            nxt.append(_dot(ktile, qall[:, cg[c]]))
            accumulate_c(jnp.maximum(i - 1, 0), alphas[c], c)
            mn, al = softmax_c(s_ref[:, cg[c]], ms[c], c)
            new_m.append(mn)
            new_a.append(al)
        for c in range(ncg):
            s_ref[:, cg[c]] = nxt[c]
        return tuple(new_m), tuple(new_a)

    ms, alphas = lax.fori_loop(0, j, body, (m0, a0))
    for c in range(ncg):
        accumulate_c(jnp.maximum(j - 1, 0), alphas[c], c)
    key_i = lax.broadcasted_iota(jnp.int32, (blk, blk), 0)
    qry_i = lax.broadcasted_iota(jnp.int32, (blk, blk), 1)
    causal = key_i <= qry_i
    off = pl.multiple_of(j * kt, kt)
    for c in range(ncg):
        hh, a = c // 2, c % 2
        parts = [jnp.where(causal, s_ref[0:blk, cg[c]], -jnp.inf)] if a == 0 else [
            s_ref[0:blk, cg[c]], jnp.where(causal, s_ref[blk:kt, cg[c]], -jnp.inf)]
        mn = ms[c]
        for st in parts:
            mn = jnp.maximum(mn, jnp.max(st, axis=0, keepdims=True))
        for r, st in enumerate(parts):
            p_ref[r * blk:(r + 1) * blk, cg[c]] = jnp.exp(st - mn).astype(BF16)
        nk = len(parts) * blk
        acc_ref[hh, :, a * blk:(a + 1) * blk] = (
            jnp.exp(ms[c] - mn) * acc_ref[hh, :, a * blk:(a + 1) * blk]
            + _dot(vt_ref[hh, :, pl.ds(off, nk)], p_ref[0:nk, cg[c]]))

    for a in range(2):
        ot = []
        for hh in range(2):
            acc = acc_ref[hh, :, a * blk:(a + 1) * blk]
            ot.append(acc[0:HEAD_DIM] * (1.0 / acc[HEAD_DIM:HEAD_DIM + 1]))
        o_ref[0, a * blk:(a + 1) * blk, :] = jnp.concatenate(ot, axis=0).T.astype(o_ref.dtype)


def _moba_tables(nblk):
    slopes = 2.0 ** (-8.0 * np.arange(1, N_MOBA_HEADS + 1) / N_MOBA_HEADS)
    npair = N_MOBA_HEADS // 2
    kaux = np.zeros((npair, MOBA_BLOCK, 2 * HEAD_DIM), np.float32)
    s256 = np.zeros((npair, 8, 2 * MOBA_BLOCK), np.float32)
    for p in range(npair):
        for hh in range(2):
            kaux[p, :, nblk + hh] = slopes[2 * p + hh] * np.arange(MOBA_BLOCK)
            s256[p, hh, :] = slopes[2 * p + hh] * MOBA_BLOCK
    return jnp.asarray(kaux, BF16), jnp.asarray(s256, F32)


def _moba(q, k, v):
    b, t, w = q.shape
    blk = MOBA_BLOCK
    nblk = t // blk
    assert blk == 256 and t % (2 * blk) == 0 and nblk % 8 == 0 and nblk <= HEAD_DIM
    hd2 = 2 * HEAD_DIM
    npair = w // hd2
    tq = 2 * blk
    kaux, s256 = _moba_tables(nblk)
    kv_spec = pl.BlockSpec((1, t, hd2), lambda bi, p, i: (bi, 0, p))
    return pl.pallas_call(
        functools.partial(_moba_kernel, nblk=nblk, blk=blk),
        grid=(b, npair, t // tq),
        in_specs=[pl.BlockSpec((1, tq, hd2), lambda bi, p, i: (bi, i, p)),
                  kv_spec, kv_spec,
                  pl.BlockSpec((1, blk, hd2), lambda bi, p, i: (p, 0, 0)),
                  pl.BlockSpec((1, 8, tq), lambda bi, p, i: (p, 0, 0))],
        out_specs=pl.BlockSpec((1, tq, hd2), lambda bi, p, i: (bi, i, p)),
        out_shape=jax.ShapeDtypeStruct((b, t, w), BF16),
        scratch_shapes=[pltpu.VMEM((t, 2 * hd2), BF16),
                        pltpu.VMEM((2, VT_ROWS, t), BF16),
                        pltpu.VMEM((nblk, hd2), F32),
                        pltpu.VMEM((2 * blk, 2 * tq), F32),
                        pltpu.VMEM((2 * blk, 2 * tq), BF16),
                        pltpu.VMEM((2, VT_ROWS, tq), F32)],
        compiler_params=_params("parallel", "parallel", "arbitrary"),
        name="moba",
    )(q, k, v, kaux, s256)


def _gla_kernel(qk_ref, v_ref, go_ref, gl_ref, wg_ref, bg_ref, gn_ref, o_ref, st_ref,
                *, sub, chunk):
    kw = N_GLA_HEADS * GLA_DK
    vw = N_GLA_HEADS * GLA_DV
    tt = qk_ref.shape[1]

    @pl.when(pl.program_id(1) == 0)
    def _reset():
        st_ref[...] = jnp.zeros_like(st_ref)

    r = lax.broadcasted_iota(jnp.int32, (sub, sub), 0)
    c = lax.broadcasted_iota(jnp.int32, (sub, sub), 1)
    same = (r >> 6) == (c >> 6)
    causal = same & (c <= r)
    ltri = jnp.where(causal, 1.0, 0.0).astype(BF16)
    eblk = jnp.where(same, 1.0, 0.0).astype(BF16)
    st_diag = ((lax.broadcasted_iota(jnp.int32, (vw, kw), 0) >> 6)
               == (lax.broadcasted_iota(jnp.int32, (vw, kw), 1) >> 5))
    grp = jnp.where((lax.broadcasted_iota(jnp.int32, (vw, vw), 0) >> 6)
                    == (lax.broadcasted_iota(jnp.int32, (vw, vw), 1) >> 6), 1.0, 0.0).astype(BF16)
    khead = lax.broadcasted_iota(jnp.int32, (sub, kw), 1) >> 5
    vhead = lax.broadcasted_iota(jnp.int32, (sub, vw), 1) >> 6

    for s in range(tt // sub):
        rows = slice(s * sub, (s + 1) * sub)
        z = _dot(gl_ref[0, rows, :], wg_ref[...]) + bg_ref[...]
        la = (jnp.minimum(z, 0.0) - jnp.log(1.0 + jnp.exp(-jnp.abs(z)))) * (1.0 / GLA_GATE_NORMALIZER)
        la_hi, la_lo = _split_bf16(la)
        b = _dot(ltri, la_hi) + _dot(ltri, la_lo)
        bl = _dot(eblk, la_hi) + _dot(eblk, la_lo)
        q = qk_ref[0, rows, 0:kw].astype(F32)
        k = qk_ref[0, rows, kw:2 * kw].astype(F32)
        v = v_ref[0, rows, :]
        qd = (q * (jnp.exp(b) * (GLA_DK ** -0.5))).astype(BF16)
        kd = (k * jnp.exp(-b)).astype(BF16)
        kend = (k * jnp.exp(bl - b)).astype(BF16)

        o = jnp.zeros((sub, vw), F32)
        for h in range(N_GLA_HEADS):
            a = _dot_nt(jnp.where(khead == h, qd, jnp.zeros_like(qd)), kd)
            a = jnp.where(causal, a, 0.0).astype(BF16)
            o = o + _dot(a, jnp.where(vhead == h, v, jnp.zeros_like(v)))

        inter = []
        for ci in range(sub // chunk):
            cr = slice(ci * chunk, (ci + 1) * chunk)
            st = st_ref[...]
            inter.append(_dot_nt(qd[cr], st.astype(BF16)))
            ut = _dot_tn(v[cr], kend[cr])
            decay = jnp.exp(bl[ci * chunk:ci * chunk + 1, :])
            st_ref[...] = st * decay + jnp.where(st_diag, ut, 0.0)
        o = o + jnp.concatenate(inter, axis=0)

        o2_hi, o2_lo = _split_bf16(o * o)
        ms = (_dot(o2_hi, grp) + _dot(o2_lo, grp)) * (1.0 / GLA_DV)
        go = go_ref[0, rows, :].astype(F32)
        y = (o * lax.rsqrt(ms + EPS)) * gn_ref[...] * (go * (1.0 / (1.0 + jnp.exp(-go))))
        o_ref[0, rows, :] = y.astype(o_ref.dtype)


def _gla(qk, v, go, gl, wg, bg, gn, tt):
    b, t, _ = qk.shape
    kw = N_GLA_HEADS * GLA_DK
    vw = N_GLA_HEADS * GLA_DV
    spec = lambda w: pl.BlockSpec((1, tt, w), lambda bi, i: (bi, i, 0))
    return pl.pallas_call(
        functools.partial(_gla_kernel, sub=4 * GLA_CHUNK, chunk=GLA_CHUNK),
        grid=(b, t // tt),
        in_specs=[spec(2 * kw), spec(vw), spec(vw), spec(LANES),
                  _resident(wg.shape), _resident(bg.shape), _resident(gn.shape)],
        out_specs=spec(vw),
        out_shape=jax.ShapeDtypeStruct((b, t, vw), BF16),
        scratch_shapes=[pltpu.VMEM((vw, kw), F32)],
        compiler_params=_params("parallel", "arbitrary"),
        name="gla",
    )(qk, v, go, gl, wg, bg, gn)


def _mem_kv_kernel(mem_ref, g_ref, w_ref, k_ref, v_ref):
    h = _rms(mem_ref[0], g_ref[0]).astype(BF16)
    kv = _dot(h, w_ref[0])
    wd = k_ref.shape[-1]
    k_ref[0, 0] = kv[:, :wd].astype(BF16)
    v_ref[0, 0] = kv[:, wd:].astype(BF16)


def _mem_kv(mem, g, wkv):
    b, m, d = mem.shape
    nl = wkv.shape[0]
    wd = wkv.shape[-1] // 2
    out = jax.ShapeDtypeStruct((nl, b, m, wd), BF16)
    ospec = pl.BlockSpec((1, 1, m, wd), lambda l, bi: (l, bi, 0, 0))
    return pl.pallas_call(
        _mem_kv_kernel,
        grid=(nl, b),
        in_specs=[pl.BlockSpec((1, m, d), lambda l, bi: (bi, 0, 0)),
                  pl.BlockSpec((1, 1, d), lambda l, bi: (l, 0, 0)),
                  pl.BlockSpec((1, d, 2 * wd), lambda l, bi: (l, 0, 0))],
        out_specs=[ospec, ospec],
        out_shape=[out, out],
        compiler_params=_params("parallel", "parallel"),
        name="mem_kv",
    )(mem, g, wkv)


def _mix_xattn_kernel(x_ref, c3_ref, c3h_ref, ym_ref, yg_ref, wout_ref, cw_ref, cb_ref,
                      gx_ref, wq_ref, mk_ref, mv_ref, wo_ref, o_ref, *, tiles_per_seq):
    tm = x_ref.shape[0]
    cc = cw_ref.shape[-1]
    nh = N_XATTN_HEADS
    first = (pl.program_id(0) % tiles_per_seq) == 0

    c3 = c3_ref[...].astype(F32)
    u = c3[:, cc:2 * cc] * c3[:, 2 * cc:3 * cc]
    halo = c3h_ref[...].astype(F32)
    uh = halo[:, cc:2 * cc] * halo[:, 2 * cc:3 * cc]
    uh = jnp.where(first, 0.0, uh)
    last1 = uh[BF16_SUBLANES - 1:BF16_SUBLANES, :]
    last2 = uh[BF16_SUBLANES - 2:BF16_SUBLANES - 1, :]
    row = lax.broadcasted_iota(jnp.int32, (tm, cc), 0)
    u1 = jnp.where(row == 0, last1, pltpu.roll(u, 1, 0))
    u2 = jnp.where(row == 0, last2, jnp.where(row == 1, last1, pltpu.roll(u, 2, 0)))
    cw = cw_ref[...]
    yc = c3[:, 0:cc] * (cw[0:1] * u2 + cw[1:2] * u1 + cw[2:3] * u + cb_ref[...])

    mw = ym_ref.shape[-1]
    x1 = (x_ref[...]
          + _dot(yc.astype(BF16), wout_ref[0:cc, :])
          + _dot(ym_ref[...], wout_ref[cc:cc + mw, :])
          + _dot(yg_ref[...], wout_ref[cc + mw:, :]))

    h = _rms(x1, gx_ref[...]).astype(BF16)
    q = _dot(h, wq_ref[...]).astype(BF16)
    k = mk_ref[0, 0]
    v = mv_ref[0, 0]
    qhead = lax.broadcasted_iota(jnp.int32, q.shape, 1) >> 6
    vhead = lax.broadcasted_iota(jnp.int32, v.shape, 1) >> 6
    o = jnp.zeros(q.shape, F32)
    for hd in range(nh):
        s = _dot_nt(jnp.where(qhead == hd, q, jnp.zeros_like(q)), k)
        p = jnp.exp(s - jnp.max(s, axis=-1, keepdims=True))
        l = jnp.sum(p, axis=-1, keepdims=True)
        pv = _dot(p.astype(BF16), jnp.where(vhead == hd, v, jnp.zeros_like(v)))
        o = o + pv * (1.0 / l)
    o_ref[...] = x1 + _dot(o.astype(BF16), wo_ref[...])


def _mix_xattn(x, c3, ym, yg, wout, cw, cb, gx, wq, memk_l, memv_l, wo, *, tm, seq):
    n, d = x.shape
    tiles_per_seq = seq // tm
    hb = tm // BF16_SUBLANES
    row = lambda w: pl.BlockSpec((tm, w), lambda i: (i, 0))
    mem_spec = pl.BlockSpec((1, 1) + memk_l.shape[2:], lambda i: (0, i // tiles_per_seq, 0, 0))
    return pl.pallas_call(
        functools.partial(_mix_xattn_kernel, tiles_per_seq=tiles_per_seq),
        grid=(n // tm,),
        in_specs=[row(d), row(c3.shape[1]),
                  pl.BlockSpec((BF16_SUBLANES, c3.shape[1]),
                               lambda i: (jnp.maximum(i * hb - 1, 0), 0)),
                  row(ym.shape[1]), row(yg.shape[1]),
                  _resident(wout.shape), _resident(cw.shape), _resident(cb.shape),
                  _resident(gx.shape), _resident(wq.shape),
                  mem_spec, mem_spec, _resident(wo.shape)],
        out_specs=row(d),
        out_shape=jax.ShapeDtypeStruct((n, d), F32),
        compiler_params=_params("parallel"),
        name="mix_xattn",
    )(x, c3, c3, ym, yg, wout, cw, cb, gx, wq, memk_l, memv_l, wo)


def _ffn_kernel(x_ref, g_ref, wa_ref, wg_ref, cwa_ref, cwg_ref, cba_ref, cbg_ref, wd_ref,
                gf_ref, o_ref, hext_ref, acc_ref, ua_ref, ug_ref, *, tiles_per_seq, final, nsub):
    tm = x_ref.shape[0]
    halo = BF16_SUBLANES
    tn = ua_ref.shape[1]
    nc = wa_ref.shape[1] // tn
    rs = tm // nsub

    def cols(j):
        return pl.ds(pl.multiple_of(j * tn, tn), tn)
    first = (pl.program_id(0) % tiles_per_seq) == 0

    @pl.when(first)
    def _zero_halo():
        hext_ref[0:halo, :] = jnp.zeros((halo, hext_ref.shape[1]), BF16)

    @pl.when(jnp.logical_not(first))
    def _carry_halo():
        hext_ref[0:halo, :] = hext_ref[tm:tm + halo, :]

    x = x_ref[...]
    hext_ref[halo:, :] = _rms(x, g_ref[...]).astype(BF16)
    acc_ref[...] = jnp.zeros_like(acc_ref)

    def up_rows(j, lo, hi):
        hrows = hext_ref[lo:hi, :]
        return _dot(hrows, wa_ref[:, cols(j)]), _dot(hrows, wg_ref[:, cols(j)])

    def conv(u, cw, cb):
        y = cw[0:1] * pltpu.roll(u, 2, 0) + cw[1:2] * pltpu.roll(u, 1, 0) + cw[2:3] * u + cb
        return y[F32_SUBLANES:]

    def down_rows(j, r):
        src = slice(halo + r * rs - F32_SUBLANES, halo + (r + 1) * rs)
        a = conv(ua_ref[src, :], cwa_ref[:, cols(j)], cba_ref[:, cols(j)])
        g = conv(ug_ref[src, :], cwg_ref[:, cols(j)], cbg_ref[:, cols(j)])
        hid = ((a * (1.0 / (1.0 + jnp.exp(-a)))) * g).astype(BF16)
        acc_ref[r * rs:(r + 1) * rs, :] += _dot(hid, wd_ref[cols(j), :])

    bounds = [(0 if r == 0 else halo + r * rs, halo + (r + 1) * rs) for r in range(nsub)]
    ua_ref[...], ug_ref[...] = up_rows(0, 0, tm + halo)

    def body(j, carry):
        nxt = []
        for r, (lo, hi) in enumerate(bounds):
            nxt.append(up_rows(j + 1, lo, hi))
            down_rows(j, r)
        for (lo, hi), (ua, ug) in zip(bounds, nxt):
            ua_ref[lo:hi, :] = ua
            ug_ref[lo:hi, :] = ug
        return carry

    lax.fori_loop(0, nc - 1, body, 0)
    for r in range(nsub):
        down_rows(nc - 1, r)
    y = x + acc_ref[...]
    if final:
        y = _rms(y, gf_ref[...])
    o_ref[...] = y


def _ffn(x, g, wa, wg, cwa, cwg, cba, cbg, wd, gf, *, tm, tn, nsub, seq, final):
    n, d = x.shape
    return pl.pallas_call(
        functools.partial(_ffn_kernel, tiles_per_seq=seq // tm, final=final, nsub=nsub),
        grid=(n // tm,),
        in_specs=[pl.BlockSpec((tm, d), lambda i: (i, 0)), _resident(g.shape),
                  _resident(wa.shape), _resident(wg.shape),
                  _resident(cwa.shape), _resident(cwg.shape),
                  _resident(cba.shape), _resident(cbg.shape),
                  _resident(wd.shape), _resident(gf.shape)],
        out_specs=pl.BlockSpec((tm, d), lambda i: (i, 0)),
        out_shape=jax.ShapeDtypeStruct((n, d), F32),
        scratch_shapes=[pltpu.VMEM((tm + BF16_SUBLANES, d), BF16),
                        pltpu.VMEM((tm, d), F32),
                        pltpu.VMEM((tm + BF16_SUBLANES, tn), F32),
                        pltpu.VMEM((tm + BF16_SUBLANES, tn), F32)],
        compiler_params=_params("arbitrary"),
        name="ffn",
    )(x, g, wa, wg, cwa, cwg, cba, cbg, wd, gf)


def _pad_cols(w, pad_to):
    return jnp.pad(w, [(0, 0)] * (w.ndim - 1) + [(0, pad_to - w.shape[-1])])


def kernel(x, mem, norm_mix_g, w_in, sc_conv_w, sc_conv_b, gla_w_gate, gla_b_gate, gla_norm_g,
           w_out, norm_xattn_g, norm_mem_g, xattn_wq, xattn_wkv, xattn_wo, norm_ffn_g,
           ffn_w_up, ffn_conv_w, ffn_conv_b, ffn_w_down, final_norm_g):
    bsz, seq, d = x.shape
    nl = w_in.shape[0]
    n = bsz * seq
    conv_ch = sc_conv_w.shape[-1]
    moba_w = N_MOBA_HEADS * HEAD_DIM
    gk_w = N_GLA_HEADS * GLA_DK
    gv_w = N_GLA_HEADS * GLA_DV
    d_ff = ffn_w_down.shape[1]
    tm = 512
    ff_tn = 256
    ff_pad = -(-d_ff // ff_tn) * ff_tn

    widths = (3 * conv_ch, moba_w, moba_w, moba_w, 2 * gk_w, gv_w, gv_w, LANES)
    in_w = w_in.shape[-1]
    q0 = 3 * conv_ch
    col_scale = jnp.ones((in_w,), F32).at[q0:q0 + moba_w].set(HEAD_DIM ** -0.5)
    w_in_p = jnp.pad(w_in * col_scale, ((0, 0), (0, 0), (0, sum(widths) - in_w))).astype(BF16)
    wg_p = jnp.pad(gla_w_gate, ((0, 0), (0, LANES - GLA_GATE_RANK), (0, 0))).astype(BF16)
    gn_t = jnp.tile(gla_norm_g, (1, N_GLA_HEADS))
    w_out_b = w_out.astype(BF16)
    wq_b = (xattn_wq * (HEAD_DIM ** -0.5)).astype(BF16)
    wkv_b = xattn_wkv.astype(BF16)
    wo_b = xattn_wo.astype(BF16)
    wa = _pad_cols(ffn_w_up[..., :d_ff], ff_pad).astype(BF16)
    wgt = _pad_cols(ffn_w_up[..., d_ff:], ff_pad).astype(BF16)
    cwa = _pad_cols(ffn_conv_w[..., :d_ff], ff_pad)
    cwg = _pad_cols(ffn_conv_w[..., d_ff:], ff_pad)
    cba = _pad_cols(ffn_conv_b[:, None, :d_ff], ff_pad)
    cbg = _pad_cols(ffn_conv_b[:, None, d_ff:], ff_pad)
    wd = jnp.pad(ffn_w_down, ((0, 0), (0, ff_pad - d_ff), (0, 0))).astype(BF16)

    memk, memv = _mem_kv(mem, norm_mem_g[:, None, :], wkv_b)

    xf = x.reshape(n, d)
    for l in range(nl):
        c3, mq, mk, mv, gqk, gv, go, gl = _norm_inproj(
            xf, norm_mix_g[l][None, :], w_in_p[l], widths, tm)
        r3 = lambda a: a.reshape(bsz, seq, a.shape[-1])
        ym = _moba(r3(mq), r3(mk), r3(mv)).reshape(n, moba_w)
        yg = _gla(r3(gqk), r3(gv), r3(go), r3(gl), wg_p[l], gla_b_gate[l][None, :],
                  gn_t[l][None, :], tt=1024).reshape(n, gv_w)
        xf = _mix_xattn(xf, c3, ym, yg, w_out_b[l], sc_conv_w[l], sc_conv_b[l][None, :],
                        norm_xattn_g[l][None, :], wq_b[l], memk[l:l + 1], memv[l:l + 1], wo_b[l],
                        tm=tm, seq=seq)
        ffn_tm, ffn_nsub = FFN_LAYER_CONFIG[l % len(FFN_LAYER_CONFIG)]
        xf = _ffn(xf, norm_ffn_g[l][None, :], wa[l], wgt[l], cwa[l], cwg[l], cba[l], cbg[l], wd[l],
                  final_norm_g[None, :], tm=ffn_tm, tn=ff_tn, nsub=ffn_nsub, seq=seq,
                  final=(l == nl - 1))
    return xf.reshape(bsz, seq, d)
```

```python
import functools

import numpy as np
import jax
import jax.numpy as jnp
from jax import lax
from jax.experimental import pallas as pl
from jax.experimental.pallas import tpu as pltpu

F32 = jnp.float32
BF16 = jnp.bfloat16

EPS = 1e-6
HEAD_DIM = 64
N_MOBA_HEADS = 8
MOBA_BLOCK = 256
MOBA_TOPK = 3
N_GLA_HEADS = 4
GLA_DK = 32
GLA_DV = 64
GLA_CHUNK = 64
GLA_GATE_RANK = 16
GLA_GATE_NORMALIZER = 16.0
CONV_K = 3
N_XATTN_HEADS = 4

LANES = 128
F32_SUBLANES = 8
BF16_SUBLANES = 16
VMEM_LIMIT = 56 * 1024 * 1024
FFN_TOKEN_TILE = 1024
TOKEN_TILE_CONFIG = (512, 512, 1024, 1024)

VT_ROWS = HEAD_DIM + 16
MASK_BIAS = -(2.0 ** 60)

_NT = (((1,), (1,)), ((), ()))
_TN = (((0,), (0,)), ((), ()))


def _dot(a, b):
    return jnp.dot(a, b, preferred_element_type=F32)


def _dot_nt(a, b):
    return lax.dot_general(a, b, _NT, preferred_element_type=F32)


def _dot_tn(a, b):
    return lax.dot_general(a, b, _TN, preferred_element_type=F32)


def _split_bf16(a):
    hi = a.astype(BF16)
    lo = (a - hi.astype(F32)).astype(BF16)
    return hi, lo


def _rms(x, g):
    r = lax.rsqrt(jnp.mean(x * x, axis=-1, keepdims=True) + EPS)
    return (x * r) * g


def _params(*sem):
    return pltpu.CompilerParams(dimension_semantics=sem, vmem_limit_bytes=VMEM_LIMIT)


def _resident(shape):
    nd = len(shape)
    return pl.BlockSpec(shape, lambda *_: (0,) * nd, pipeline_mode=pl.Buffered(1))


def _layer(arr, l):
    nd = arr.ndim
    return pl.BlockSpec((None,) + arr.shape[1:], lambda *_: (l,) + (0,) * (nd - 1),
                        pipeline_mode=pl.Buffered(1))


def _norm_inproj_kernel(x_ref, g_ref, w_ref, *out_refs):
    h = _rms(x_ref[...], g_ref[...]).astype(BF16)
    off = 0
    for o_ref in out_refs:
        if len(o_ref.shape) == 3:
            wd = o_ref.shape[0] * o_ref.shape[2]
            res = _dot(h, w_ref[:, off:off + wd]).astype(o_ref.dtype)
            for p in range(o_ref.shape[0]):
                o_ref[p] = res[:, p * o_ref.shape[2]:(p + 1) * o_ref.shape[2]]
        else:
            wd = o_ref.shape[-1]
            o_ref[...] = _dot(h, w_ref[:, off:off + wd]).astype(o_ref.dtype)
        off += wd


def _norm_inproj(x, g, w, l, widths, pair_major, tm):
    n, d = x.shape
    specs, shapes = [], []
    for wd, pm in zip(widths, pair_major):
        if pm:
            specs.append(pl.BlockSpec((wd // LANES, tm, LANES), lambda i: (0, i, 0)))
            shapes.append(jax.ShapeDtypeStruct((wd // LANES, n, LANES), BF16))
        else:
            specs.append(pl.BlockSpec((tm, wd), lambda i: (i, 0)))
            shapes.append(jax.ShapeDtypeStruct((n, wd), BF16))
    return pl.pallas_call(
        _norm_inproj_kernel,
        grid=(n // tm,),
        in_specs=[pl.BlockSpec((tm, d), lambda i: (i, 0)), _layer(g, l), _layer(w, l)],
        out_specs=specs,
        out_shape=shapes,
        compiler_params=_params("parallel"),
        name="norm_inproj",
    )(x, g, w)


def _moba_kernel(q_ref, k_ref, v_ref, kaux_ref, s256_ref, o_ref,
                 kaug_ref, vt_ref, kmean_ref, s_ref, p_ref, acc_ref, *, nblk, blk):
    j = pl.program_id(2)
    hd2 = 2 * HEAD_DIM
    tq = 2 * blk
    kt = 2 * blk

    @pl.when(j == 0)
    def _prepare():
        lane = lax.broadcasted_iota(jnp.int32, (blk, hd2), 1)
        aux = kaux_ref[0].astype(F32)
        ones = jnp.ones((VT_ROWS - HEAD_DIM, blk), BF16)
        for n in range(nblk):
            rows = slice(n * blk, (n + 1) * blk)
            kb = k_ref[rows, :]
            kaug_ref[rows, 0:hd2] = kb
            kaug_ref[rows, hd2:2 * hd2] = (aux + jnp.where(lane == n, 1.0, 0.0)).astype(BF16)
            vt = v_ref[rows, :].astype(F32).T.astype(BF16)
            for hh in range(2):
                vt_ref[hh, 0:HEAD_DIM, rows] = vt[hh * HEAD_DIM:(hh + 1) * HEAD_DIM]
                vt_ref[hh, HEAD_DIM:VT_ROWS, rows] = ones
            kmean_ref[n:n + 1, :] = jnp.sum(kb.astype(F32), axis=0, keepdims=True) * (1.0 / blk)

    qt = q_ref[...].astype(F32).T
    qt_b = qt.astype(BF16)
    km = kmean_ref[...]
    km_lane_head = lax.broadcasted_iota(jnp.int32, (nblk, hd2), 1) >> 6
    qt_row_head = lax.broadcasted_iota(jnp.int32, (hd2, tq), 0) >> 6
    nidx = lax.broadcasted_iota(jnp.int32, (nblk, tq), 0)
    qblk = 2 * j + (lax.broadcasted_iota(jnp.int32, (nblk, tq), 1) >> 8)
    aux_row = lax.broadcasted_iota(jnp.int32, (hd2 - nblk, tq), 0)
    past = nidx < qblk

    qaug = []
    for hh in range(2):
        kmh = jnp.where(km_lane_head == hh, km, 0.0)
        kh, kl = _split_bf16(kmh)
        gate = _dot(kh, qt_b) + _dot(kl, qt_b)
        g = jnp.where(past, gate, -jnp.inf)
        rank = jnp.zeros((nblk, tq), F32)
        for m in range(nblk):
            gm = g[m:m + 1, :]
            rank = rank + jnp.where(m < nidx, jnp.where(gm >= g, 1.0, 0.0), jnp.where(gm > g, 1.0, 0.0))
        sel = (past & (rank < MOBA_TOPK)) | (nidx == qblk)
        s256 = s256_ref[0, hh:hh + 1, :]
        bias = jnp.where(sel, (nidx - qblk).astype(F32) * s256, MASK_BIAS)
        head_one = jnp.where(aux_row == hh, 1.0, 0.0)
        qaug.append(jnp.concatenate(
            [jnp.where(qt_row_head == hh, qt, 0.0), bias, head_one], axis=0).astype(BF16))
    qall = jnp.concatenate(qaug, axis=1)

    ncg = 4
    cg = [slice(c * blk, (c + 1) * blk) for c in range(ncg)]

    def softmax_c(st, m, c):
        mn = jnp.maximum(m, jnp.max(st, axis=0, keepdims=True))
        p_ref[:, cg[c]] = jnp.exp(st - mn).astype(BF16)
        return mn, jnp.exp(m - mn)

    def accumulate_c(n, alpha, c):
        off = pl.multiple_of(n * kt, kt)
        hh, a = c // 2, c % 2
        acc_ref[hh, :, a * blk:(a + 1) * blk] = (
            alpha * acc_ref[hh, :, a * blk:(a + 1) * blk]
            + _dot(vt_ref[hh, :, pl.ds(off, kt)], p_ref[:, cg[c]]))

    s_ref[...] = _dot(kaug_ref[0:kt, :], qall)
    p_ref[...] = jnp.zeros_like(p_ref)
    acc_ref[...] = jnp.zeros_like(acc_ref)
    m0 = tuple(jnp.full((1, blk), -jnp.inf, F32) for _ in range(ncg))
    a0 = tuple(jnp.ones((1, blk), F32) for _ in range(ncg))

    def body(i, carry):
        ms, alphas = carry
        off = pl.multiple_of((i + 1) * kt, kt)
        ktile = kaug_ref[pl.ds(off, kt), :]
        new_m, new_a, nxt = [], [], []
        for c in range(ncg):
            nxt.append(_dot(ktile, qall[:, cg[c]]))
            accumulate_c(jnp.maximum(i - 1, 0), alphas[c], c)
            mn, al = softmax_c(s_ref[:, cg[c]], ms[c], c)
            new_m.append(mn)
            new_a.append(al)
        for c in range(ncg):
            s_ref[:, cg[c]] = nxt[c]
        return tuple(new_m), tuple(new_a)

    ms, alphas = lax.fori_loop(0, j, body, (m0, a0))
    for c in range(ncg):
        accumulate_c(jnp.maximum(j - 1, 0), alphas[c], c)
    key_i = lax.broadcasted_iota(jnp.int32, (blk, blk), 0)
    qry_i = lax.broadcasted_iota(jnp.int32, (blk, blk), 1)
    causal = key_i <= qry_i
    off = pl.multiple_of(j * kt, kt)
    for c in range(ncg):
        hh, a = c // 2, c % 2
        parts = [jnp.where(causal, s_ref[0:blk, cg[c]], -jnp.inf)] if a == 0 else [
            s_ref[0:blk, cg[c]], jnp.where(causal, s_ref[blk:kt, cg[c]], -jnp.inf)]
        mn = ms[c]
        for st in parts:
            mn = jnp.maximum(mn, jnp.max(st, axis=0, keepdims=True))
        for r, st in enumerate(parts):
            p_ref[r * blk:(r + 1) * blk, cg[c]] = jnp.exp(st - mn).astype(BF16)
        nk = len(parts) * blk
        acc_ref[hh, :, a * blk:(a + 1) * blk] = (
            jnp.exp(ms[c] - mn) * acc_ref[hh, :, a * blk:(a + 1) * blk]
            + _dot(vt_ref[hh, :, pl.ds(off, nk)], p_ref[0:nk, cg[c]]))

    for a in range(2):
        ot = []
        for hh in range(2):
            acc = acc_ref[hh, :, a * blk:(a + 1) * blk]
            ot.append(acc[0:HEAD_DIM] * (1.0 / acc[HEAD_DIM:HEAD_DIM + 1]))
        o_ref[a * blk:(a + 1) * blk, :] = jnp.concatenate(ot, axis=0).T.astype(o_ref.dtype)


def _moba_tables(nblk):
    slopes = 2.0 ** (-8.0 * np.arange(1, N_MOBA_HEADS + 1) / N_MOBA_HEADS)
    npair = N_MOBA_HEADS // 2
    kaux = np.zeros((npair, MOBA_BLOCK, 2 * HEAD_DIM), np.float32)
    s256 = np.zeros((npair, 8, 2 * MOBA_BLOCK), np.float32)
    for p in range(npair):
        for hh in range(2):
            kaux[p, :, nblk + hh] = slopes[2 * p + hh] * np.arange(MOBA_BLOCK)
            s256[p, hh, :] = slopes[2 * p + hh] * MOBA_BLOCK
    return jnp.asarray(kaux, BF16), jnp.asarray(s256, F32)


def _moba(q, k, v, bsz):
    npair, n, hd2 = q.shape
    t = n // bsz
    blk = MOBA_BLOCK
    nblk = t // blk
    assert blk == 256 and t % (2 * blk) == 0 and nblk % 8 == 0 and nblk <= HEAD_DIM
    assert hd2 == 2 * HEAD_DIM
    tq = 2 * blk
    steps = t // tq
    kaux, s256 = _moba_tables(nblk)
    q_spec = pl.BlockSpec((None, tq, hd2), lambda bi, p, i: (p, bi * steps + i, 0))
    kv_spec = pl.BlockSpec((None, t, hd2), lambda bi, p, i: (p, bi, 0))
    return pl.pallas_call(
        functools.partial(_moba_kernel, nblk=nblk, blk=blk),
        grid=(bsz, npair, steps),
        in_specs=[q_spec, kv_spec, kv_spec,
                  pl.BlockSpec((1, blk, hd2), lambda bi, p, i: (p, 0, 0)),
                  pl.BlockSpec((1, 8, tq), lambda bi, p, i: (p, 0, 0))],
        out_specs=q_spec,
        out_shape=jax.ShapeDtypeStruct((npair, n, hd2), BF16),
        scratch_shapes=[pltpu.VMEM((t, 2 * hd2), BF16),
                        pltpu.VMEM((2, VT_ROWS, t), BF16),
                        pltpu.VMEM((nblk, hd2), F32),
                        pltpu.VMEM((2 * blk, 2 * tq), F32),
                        pltpu.VMEM((2 * blk, 2 * tq), BF16),
                        pltpu.VMEM((2, VT_ROWS, tq), F32)],
        compiler_params=_params("parallel", "parallel", "arbitrary"),
        name="moba",
    )(q, k, v, kaux, s256)


def _gla_kernel(qk_ref, v_ref, go_ref, gl_ref, wg_ref, bg_ref, gn_ref, o_ref, st_ref,
                *, sub, chunk):
    kw = N_GLA_HEADS * GLA_DK
    vw = N_GLA_HEADS * GLA_DV
    tt = qk_ref.shape[1]

    @pl.when(pl.program_id(1) == 0)
    def _reset():
        st_ref[...] = jnp.zeros_like(st_ref)

    r = lax.broadcasted_iota(jnp.int32, (sub, sub), 0)
    c = lax.broadcasted_iota(jnp.int32, (sub, sub), 1)
    same = (r >> 6) == (c >> 6)
    causal = same & (c <= r)
    ltri = jnp.where(causal, 1.0, 0.0).astype(BF16)
    eblk = jnp.where(same, 1.0, 0.0).astype(BF16)
    st_diag = ((lax.broadcasted_iota(jnp.int32, (vw, kw), 0) >> 6)
               == (lax.broadcasted_iota(jnp.int32, (vw, kw), 1) >> 5))
    grp = jnp.where((lax.broadcasted_iota(jnp.int32, (vw, vw), 0) >> 6)
                    == (lax.broadcasted_iota(jnp.int32, (vw, vw), 1) >> 6), 1.0, 0.0).astype(BF16)
    khead = lax.broadcasted_iota(jnp.int32, (sub, kw), 1) >> 5
    vhead = lax.broadcasted_iota(jnp.int32, (sub, vw), 1) >> 6

    for s in range(tt // sub):
        rows = slice(s * sub, (s + 1) * sub)
        z = _dot(gl_ref[0, rows, :], wg_ref[...]) + bg_ref[...]
        la = (jnp.minimum(z, 0.0) - jnp.log(1.0 + jnp.exp(-jnp.abs(z)))) * (1.0 / GLA_GATE_NORMALIZER)
        la_hi, la_lo = _split_bf16(la)
        b = _dot(ltri, la_hi) + _dot(ltri, la_lo)
        bl = _dot(eblk, la_hi) + _dot(eblk, la_lo)
        q = qk_ref[0, rows, 0:kw].astype(F32)
        k = qk_ref[0, rows, kw:2 * kw].astype(F32)
        v = v_ref[0, rows, :]
        qd = (q * (jnp.exp(b) * (GLA_DK ** -0.5))).astype(BF16)
        kd = (k * jnp.exp(-b)).astype(BF16)
        kend = (k * jnp.exp(bl - b)).astype(BF16)

        o = jnp.zeros((sub, vw), F32)
        for h in range(N_GLA_HEADS):
            a = _dot_nt(jnp.where(khead == h, qd, jnp.zeros_like(qd)), kd)
            a = jnp.where(causal, a, 0.0).astype(BF16)
            o = o + _dot(a, jnp.where(vhead == h, v, jnp.zeros_like(v)))

        inter = []
        for ci in range(sub // chunk):
            cr = slice(ci * chunk, (ci + 1) * chunk)
            st = st_ref[...]
            inter.append(_dot_nt(qd[cr], st.astype(BF16)))
            ut = _dot_tn(v[cr], kend[cr])
            decay = jnp.exp(bl[ci * chunk:ci * chunk + 1, :])
            st_ref[...] = st * decay + jnp.where(st_diag, ut, 0.0)
        o = o + jnp.concatenate(inter, axis=0)

        o2_hi, o2_lo = _split_bf16(o * o)
        ms = (_dot(o2_hi, grp) + _dot(o2_lo, grp)) * (1.0 / GLA_DV)
        go = go_ref[0, rows, :].astype(F32)
        y = (o * lax.rsqrt(ms + EPS)) * gn_ref[...] * (go * (1.0 / (1.0 + jnp.exp(-go))))
        o_ref[0, rows, :] = y.astype(o_ref.dtype)


def _gla(qk, v, go, gl, wg, bg, gn, l, tt):
    b, t, _ = qk.shape
    kw = N_GLA_HEADS * GLA_DK
    vw = N_GLA_HEADS * GLA_DV
    spec = lambda w: pl.BlockSpec((1, tt, w), lambda bi, i: (bi, i, 0))
    return pl.pallas_call(
        functools.partial(_gla_kernel, sub=4 * GLA_CHUNK, chunk=GLA_CHUNK),
        grid=(b, t // tt),
        in_specs=[spec(2 * kw), spec(vw), spec(vw), spec(LANES),
                  _layer(wg, l), _layer(bg, l), _layer(gn, l)],
        out_specs=spec(vw),
        out_shape=jax.ShapeDtypeStruct((b, t, vw), BF16),
        scratch_shapes=[pltpu.VMEM((vw, kw), F32)],
        compiler_params=_params("parallel", "arbitrary"),
        name="gla",
    )(qk, v, go, gl, wg, bg, gn)


def _mem_kv_kernel(mem_ref, g_ref, w_ref, k_ref, v_ref):
    h = _rms(mem_ref[0], g_ref[0]).astype(BF16)
    kv = _dot(h, w_ref[0])
    wd = k_ref.shape[-1]
    k_ref[0, 0] = kv[:, :wd].astype(BF16)
    v_ref[0, 0] = kv[:, wd:].astype(BF16)


def _mem_kv(mem, g, wkv):
    b, m, d = mem.shape
    nl = wkv.shape[0]
    wd = wkv.shape[-1] // 2
    out = jax.ShapeDtypeStruct((nl, b, m, wd), BF16)
    ospec = pl.BlockSpec((1, 1, m, wd), lambda l, bi: (l, bi, 0, 0))
    return pl.pallas_call(
        _mem_kv_kernel,
        grid=(nl, b),
        in_specs=[pl.BlockSpec((1, m, d), lambda l, bi: (bi, 0, 0)),
                  pl.BlockSpec((1, 1, d), lambda l, bi: (l, 0, 0)),
                  pl.BlockSpec((1, d, 2 * wd), lambda l, bi: (l, 0, 0))],
        out_specs=[ospec, ospec],
        out_shape=[out, out],
        compiler_params=_params("parallel", "parallel"),
        name="mem_kv",
    )(mem, g, wkv)


def _mix_xattn_kernel(x_ref, c3_ref, c3h_ref, ym_ref, yg_ref, wout_ref, cw_ref, cb_ref,
                      gx_ref, wq_ref, mk_ref, mv_ref, wo_ref, o_ref, *, tiles_per_seq):
    tm = x_ref.shape[0]
    cc = cw_ref.shape[-1]
    nh = N_XATTN_HEADS
    first = (pl.program_id(0) % tiles_per_seq) == 0

    c3 = c3_ref[...].astype(F32)
    u = c3[:, cc:2 * cc] * c3[:, 2 * cc:3 * cc]
    halo = c3h_ref[...].astype(F32)
    uh = halo[:, cc:2 * cc] * halo[:, 2 * cc:3 * cc]
    uh = jnp.where(first, 0.0, uh)
    last1 = uh[BF16_SUBLANES - 1:BF16_SUBLANES, :]
    last2 = uh[BF16_SUBLANES - 2:BF16_SUBLANES - 1, :]
    row = lax.broadcasted_iota(jnp.int32, (tm, cc), 0)
    u1 = jnp.where(row == 0, last1, pltpu.roll(u, 1, 0))
    u2 = jnp.where(row == 0, last2, jnp.where(row == 1, last1, pltpu.roll(u, 2, 0)))
    cw = cw_ref[...]
    yc = c3[:, 0:cc] * (cw[0:1] * u2 + cw[1:2] * u1 + cw[2:3] * u + cb_ref[...])

    ym = jnp.concatenate([ym_ref[p] for p in range(ym_ref.shape[0])], axis=1)
    mw = ym.shape[-1]
    x1 = (x_ref[...]
          + _dot(yc.astype(BF16), wout_ref[0:cc, :])
          + _dot(ym, wout_ref[cc:cc + mw, :])
          + _dot(yg_ref[...], wout_ref[cc + mw:, :]))

    h = _rms(x1, gx_ref[...]).astype(BF16)
    q = _dot(h, wq_ref[...]).astype(BF16)
    k = mk_ref[...]
    v = mv_ref[...]
    qhead = lax.broadcasted_iota(jnp.int32, q.shape, 1) >> 6
    vhead = lax.broadcasted_iota(jnp.int32, v.shape, 1) >> 6
    o = jnp.zeros(q.shape, F32)
    for hd in range(nh):
        s = _dot_nt(jnp.where(qhead == hd, q, jnp.zeros_like(q)), k)
        p = jnp.exp(s - jnp.max(s, axis=-1, keepdims=True))
        l = jnp.sum(p, axis=-1, keepdims=True)
        pv = _dot(p.astype(BF16), jnp.where(vhead == hd, v, jnp.zeros_like(v)))
        o = o + pv * (1.0 / l)
    o_ref[...] = x1 + _dot(o.astype(BF16), wo_ref[...])


def _mix_xattn(x, c3, ym, yg, wout, cw, cb, gx, wq, memk, memv, wo, l, *, tm, seq):
    n, d = x.shape
    tiles_per_seq = seq // tm
    hb = tm // BF16_SUBLANES
    row = lambda w: pl.BlockSpec((tm, w), lambda i: (i, 0))
    mem_spec = pl.BlockSpec((None, None) + memk.shape[2:], lambda i: (l, i // tiles_per_seq, 0, 0))
    return pl.pallas_call(
        functools.partial(_mix_xattn_kernel, tiles_per_seq=tiles_per_seq),
        grid=(n // tm,),
        in_specs=[row(d), row(c3.shape[1]),
                  pl.BlockSpec((BF16_SUBLANES, c3.shape[1]),
                               lambda i: (jnp.maximum(i * hb - 1, 0), 0)),
                  pl.BlockSpec((ym.shape[0], tm, ym.shape[2]), lambda i: (0, i, 0)),
                  row(yg.shape[1]),
                  _layer(wout, l), _layer(cw, l), _layer(cb, l), _layer(gx, l), _layer(wq, l),
                  mem_spec, mem_spec, _layer(wo, l)],
        out_specs=row(d),
        out_shape=jax.ShapeDtypeStruct((n, d), F32),
        compiler_params=_params("parallel"),
        name="mix_xattn",
    )(x, c3, c3, ym, yg, wout, cw, cb, gx, wq, memk, memv, wo)


def _ffn_kernel(x_ref, g_ref, wa_ref, wg_ref, cwa_ref, cwg_ref, cba_ref, cbg_ref, wd_ref,
                gf_ref, o_ref, hext_ref, acc_ref, ua_ref, ug_ref, *, tiles_per_seq, final, nsub):
    tm = x_ref.shape[0]
    halo = BF16_SUBLANES
    tn = ua_ref.shape[1]
    nc = wa_ref.shape[1] // tn
    rs = tm // nsub
    first = (pl.program_id(0) % tiles_per_seq) == 0

    def cols(j):
        return pl.ds(pl.multiple_of(j * tn, tn), tn)

    @pl.when(first)
    def _zero_halo():
        hext_ref[0:halo, :] = jnp.zeros((halo, hext_ref.shape[1]), BF16)

    @pl.when(jnp.logical_not(first))
    def _carry_halo():
        hext_ref[0:halo, :] = hext_ref[tm:tm + halo, :]

    x = x_ref[...]
    hext_ref[halo:, :] = _rms(x, g_ref[...]).astype(BF16)
    acc_ref[...] = jnp.zeros_like(acc_ref)

    def up_rows(j, lo, hi):
        hrows = hext_ref[lo:hi, :]
        return _dot(hrows, wa_ref[:, cols(j)]), _dot(hrows, wg_ref[:, cols(j)])

    def conv(u, cw, cb):
        y = cw[0:1] * pltpu.roll(u, 2, 0) + cw[1:2] * pltpu.roll(u, 1, 0) + cw[2:3] * u + cb
        return y[F32_SUBLANES:]

    def down_rows(j, r):
        src = slice(halo + r * rs - F32_SUBLANES, halo + (r + 1) * rs)
        a = conv(ua_ref[src, :], cwa_ref[:, cols(j)], cba_ref[:, cols(j)])
        g = conv(ug_ref[src, :], cwg_ref[:, cols(j)], cbg_ref[:, cols(j)])
        hid = ((a * (1.0 / (1.0 + jnp.exp(-a)))) * g).astype(BF16)
        acc_ref[r * rs:(r + 1) * rs, :] += _dot(hid, wd_ref[cols(j), :])

    bounds = [(0 if r == 0 else halo + r * rs, halo + (r + 1) * rs) for r in range(nsub)]
    ua_ref[...], ug_ref[...] = up_rows(0, 0, tm + halo)

    def body(j, carry):
        nxt = []
        for r, (lo, hi) in enumerate(bounds):
            nxt.append(up_rows(j + 1, lo, hi))
            down_rows(j, r)
        for (lo, hi), (ua, ug) in zip(bounds, nxt):
            ua_ref[lo:hi, :] = ua
            ug_ref[lo:hi, :] = ug
        return carry

    lax.fori_loop(0, nc - 1, body, 0)
    for r in range(nsub):
        down_rows(nc - 1, r)
    y = x + acc_ref[...]
    if final:
        y = _rms(y, gf_ref[...])
    o_ref[...] = y


def _ffn(x, g, wa, wg, cwa, cwg, cba, cbg, wd, gf, l, *, tm, tn, nsub, seq, final):
    n, d = x.shape
    return pl.pallas_call(
        functools.partial(_ffn_kernel, tiles_per_seq=seq // tm, final=final, nsub=nsub),
        grid=(n // tm,),
        in_specs=[pl.BlockSpec((tm, d), lambda i: (i, 0)), _layer(g, l),
                  _layer(wa, l), _layer(wg, l), _layer(cwa, l), _layer(cwg, l),
                  _layer(cba, l), _layer(cbg, l), _layer(wd, l), _resident(gf.shape)],
        out_specs=pl.BlockSpec((tm, d), lambda i: (i, 0)),
        out_shape=jax.ShapeDtypeStruct((n, d), F32),
        scratch_shapes=[pltpu.VMEM((tm + BF16_SUBLANES, d), BF16),
                        pltpu.VMEM((tm, d), F32),
                        pltpu.VMEM((tm + BF16_SUBLANES, tn), F32),
                        pltpu.VMEM((tm + BF16_SUBLANES, tn), F32)],
        compiler_params=_params("arbitrary"),
        name="ffn",
    )(x, g, wa, wg, cwa, cwg, cba, cbg, wd, gf)


def _pad_cols(w, pad_to):
    return jnp.pad(w, [(0, 0)] * (w.ndim - 1) + [(0, pad_to - w.shape[-1])])


def kernel(x, mem, norm_mix_g, w_in, sc_conv_w, sc_conv_b, gla_w_gate, gla_b_gate, gla_norm_g,
           w_out, norm_xattn_g, norm_mem_g, xattn_wq, xattn_wkv, xattn_wo, norm_ffn_g,
           ffn_w_up, ffn_conv_w, ffn_conv_b, ffn_w_down, final_norm_g):
    bsz, seq, d = x.shape
    nl = w_in.shape[0]
    n = bsz * seq
    conv_ch = sc_conv_w.shape[-1]
    moba_w = N_MOBA_HEADS * HEAD_DIM
    gk_w = N_GLA_HEADS * GLA_DK
    gv_w = N_GLA_HEADS * GLA_DV
    d_ff = ffn_w_down.shape[1]
    ff_tn = 256
    ff_pad = -(-d_ff // ff_tn) * ff_tn

    widths = (3 * conv_ch, moba_w, moba_w, moba_w, 2 * gk_w, gv_w, gv_w, LANES)
    in_w = w_in.shape[-1]
    q0 = 3 * conv_ch
    col_scale = jnp.ones((in_w,), F32).at[q0:q0 + moba_w].set(HEAD_DIM ** -0.5)
    w_in_p = jnp.pad(w_in * col_scale, ((0, 0), (0, 0), (0, sum(widths) - in_w))).astype(BF16)
    wg_p = jnp.pad(gla_w_gate, ((0, 0), (0, LANES - GLA_GATE_RANK), (0, 0))).astype(BF16)
    gn_t = jnp.tile(gla_norm_g, (1, N_GLA_HEADS))
    w_out_b = w_out.astype(BF16)
    wq_b = (xattn_wq * (HEAD_DIM ** -0.5)).astype(BF16)
    wkv_b = xattn_wkv.astype(BF16)
    wo_b = xattn_wo.astype(BF16)
    wa = _pad_cols(ffn_w_up[..., :d_ff], ff_pad).astype(BF16)
    wgt = _pad_cols(ffn_w_up[..., d_ff:], ff_pad).astype(BF16)
    cwa = _pad_cols(ffn_conv_w[..., :d_ff], ff_pad)
    cwg = _pad_cols(ffn_conv_w[..., d_ff:], ff_pad)
    cba = _pad_cols(ffn_conv_b[:, None, :d_ff], ff_pad)
    cbg = _pad_cols(ffn_conv_b[:, None, d_ff:], ff_pad)
    wd = jnp.pad(ffn_w_down, ((0, 0), (0, ff_pad - d_ff), (0, 0))).astype(BF16)

    memk, memv = _mem_kv(mem, norm_mem_g[:, None, :], wkv_b)

    vec = lambda a: a[:, None, :]
    norm_mix_v, norm_xattn_v, norm_ffn_v = vec(norm_mix_g), vec(norm_xattn_g), vec(norm_ffn_g)
    gla_bg_v, gn_v, sc_conv_b_v = vec(gla_b_gate), vec(gn_t), vec(sc_conv_b)
    pair_major = (False, True, True, True, False, False, False, False)

    xf = x.reshape(n, d)
    for l in range(nl):
        tm = TOKEN_TILE_CONFIG[l % len(TOKEN_TILE_CONFIG)]
        c3, mq, mk, mv, gqk, gv, go, gl = _norm_inproj(
            xf, norm_mix_v, w_in_p, l, widths, pair_major, tm)
        r3 = lambda a: a.reshape(bsz, seq, a.shape[-1])
        ym = _moba(mq, mk, mv, bsz)
        yg = _gla(r3(gqk), r3(gv), r3(go), r3(gl), wg_p, gla_bg_v, gn_v, l,
                  tt=1024).reshape(n, gv_w)
        xf = _mix_xattn(xf, c3, ym, yg, w_out_b, sc_conv_w, sc_conv_b_v, norm_xattn_v, wq_b,
                        memk, memv, wo_b, l, tm=tm, seq=seq)
        xf = _ffn(xf, norm_ffn_v, wa, wgt, cwa, cwg, cba, cbg, wd, final_norm_g[None, :], l,
                  tm=FFN_TOKEN_TILE, tn=ff_tn, nsub=1, seq=seq, final=(l == nl - 1))
    return xf.reshape(bsz, seq, d)
```

```python
import functools

import numpy as np
import jax
import jax.numpy as jnp
from jax import lax
from jax.experimental import pallas as pl
from jax.experimental.pallas import tpu as pltpu

F32 = jnp.float32
BF16 = jnp.bfloat16

EPS = 1e-6
HEAD_DIM = 64
N_MOBA_HEADS = 8
MOBA_BLOCK = 256
MOBA_TOPK = 3
N_GLA_HEADS = 4
GLA_DK = 32
GLA_DV = 64
GLA_CHUNK = 64
GLA_GATE_RANK = 16
GLA_GATE_NORMALIZER = 16.0
CONV_K = 3
N_XATTN_HEADS = 4

LANES = 128
F32_SUBLANES = 8
BF16_SUBLANES = 16
VMEM_LIMIT = 56 * 1024 * 1024
TOKEN_TILE = 1024
FFN_CHUNK = 256
GLA_TILE = 1024

VT_ROWS = HEAD_DIM + 16
MASK_BIAS = -(2.0 ** 60)

_NT = (((1,), (1,)), ((), ()))
_TN = (((0,), (0,)), ((), ()))


def _dot(a, b):
    return jnp.dot(a, b, preferred_element_type=F32)


def _dot_nt(a, b):
    return lax.dot_general(a, b, _NT, preferred_element_type=F32)


def _dot_tn(a, b):
    return lax.dot_general(a, b, _TN, preferred_element_type=F32)


def _split_bf16(a):
    hi = a.astype(BF16)
    lo = (a - hi.astype(F32)).astype(BF16)
    return hi, lo


def _rms(x, g):
    r = lax.rsqrt(jnp.mean(x * x, axis=-1, keepdims=True) + EPS)
    return (x * r) * g


def _params(*sem):
    return pltpu.CompilerParams(dimension_semantics=sem, vmem_limit_bytes=VMEM_LIMIT)


def _resident(shape):
    nd = len(shape)
    return pl.BlockSpec(shape, lambda *_: (0,) * nd, pipeline_mode=pl.Buffered(1))


def _layer(arr, l):
    nd = arr.ndim
    return pl.BlockSpec((None,) + arr.shape[1:], lambda *_: (l,) + (0,) * (nd - 1),
                        pipeline_mode=pl.Buffered(1))


def _norm_inproj_kernel(x_ref, g_ref, w_ref, wlast_ref, *out_refs, scales):
    h = _rms(x_ref[...], g_ref[...]).astype(BF16)
    off = 0
    for o_ref, scale in zip(out_refs[:-1], scales):
        pair_major = len(o_ref.shape) == 3
        wd = o_ref.shape[0] * o_ref.shape[2] if pair_major else o_ref.shape[-1]
        w = w_ref[:, off:off + wd]
        if scale != 1.0:
            w = w * scale
        res = _dot(h, w.astype(BF16)).astype(o_ref.dtype)
        if pair_major:
            for p in range(o_ref.shape[0]):
                o_ref[p] = res[:, p * o_ref.shape[2]:(p + 1) * o_ref.shape[2]]
        else:
            o_ref[...] = res
        off += wd
    out_refs[-1][...] = _dot(h, wlast_ref[...]).astype(out_refs[-1].dtype)


def _norm_inproj(x, g, w, wlast, l, widths, pair_major, scales, tm):
    n, d = x.shape
    specs, shapes = [], []
    for wd, pm in zip(widths, pair_major):
        if pm:
            specs.append(pl.BlockSpec((wd // LANES, tm, LANES), lambda i: (0, i, 0)))
            shapes.append(jax.ShapeDtypeStruct((wd // LANES, n, LANES), BF16))
        else:
            specs.append(pl.BlockSpec((tm, wd), lambda i: (i, 0)))
            shapes.append(jax.ShapeDtypeStruct((n, wd), BF16))
    return pl.pallas_call(
        functools.partial(_norm_inproj_kernel, scales=scales),
        grid=(n // tm,),
        in_specs=[pl.BlockSpec((tm, d), lambda i: (i, 0)), _layer(g, l), _layer(w, l),
                  _layer(wlast, l)],
        out_specs=specs,
        out_shape=shapes,
        compiler_params=_params("parallel"),
        name="norm_inproj",
    )(x, g, w, wlast)


def _moba_kernel(q_ref, k_ref, v_ref, kaux_ref, s256_ref, o_ref,
                 kaug_ref, vt_ref, kmean_ref, s_ref, p_ref, acc_ref, *, nblk, blk):
    j = pl.program_id(2)
    hd2 = 2 * HEAD_DIM
    tq = 2 * blk
    kt = 2 * blk

    @pl.when(j == 0)
    def _prepare():
        lane = lax.broadcasted_iota(jnp.int32, (blk, hd2), 1)
        aux = kaux_ref[0].astype(F32)
        ones = jnp.ones((VT_ROWS - HEAD_DIM, blk), BF16)
        for n in range(nblk):
            rows = slice(n * blk, (n + 1) * blk)
            kb = k_ref[rows, :]
            kaug_ref[rows, 0:hd2] = kb
            kaug_ref[rows, hd2:2 * hd2] = (aux + jnp.where(lane == n, 1.0, 0.0)).astype(BF16)
            vt = v_ref[rows, :].astype(F32).T.astype(BF16)
            for hh in range(2):
                vt_ref[hh, 0:HEAD_DIM, rows] = vt[hh * HEAD_DIM:(hh + 1) * HEAD_DIM]
                vt_ref[hh, HEAD_DIM:VT_ROWS, rows] = ones
            kmean_ref[n:n + 1, :] = jnp.sum(kb.astype(F32), axis=0, keepdims=True) * (1.0 / blk)

    qt = q_ref[...].astype(F32).T
    qt_b = qt.astype(BF16)
    km = kmean_ref[...]
    km_lane_head = lax.broadcasted_iota(jnp.int32, (nblk, hd2), 1) >> 6
    qt_row_head = lax.broadcasted_iota(jnp.int32, (hd2, tq), 0) >> 6
    nidx = lax.broadcasted_iota(jnp.int32, (nblk, tq), 0)
    qblk = 2 * j + (lax.broadcasted_iota(jnp.int32, (nblk, tq), 1) >> 8)
    aux_row = lax.broadcasted_iota(jnp.int32, (hd2 - nblk, tq), 0)
    past = nidx < qblk

    qaug = []
    for hh in range(2):
        kmh = jnp.where(km_lane_head == hh, km, 0.0)
        kh, kl = _split_bf16(kmh)
        gate = _dot(kh, qt_b) + _dot(kl, qt_b)
        g = jnp.where(past, gate, -jnp.inf)
        rank = jnp.zeros((nblk, tq), F32)
        for m in range(nblk):
            gm = g[m:m + 1, :]
            rank = rank + jnp.where(m < nidx, jnp.where(gm >= g, 1.0, 0.0), jnp.where(gm > g, 1.0, 0.0))
        sel = (past & (rank < MOBA_TOPK)) | (nidx == qblk)
        s256 = s256_ref[0, hh:hh + 1, :]
        bias = jnp.where(sel, (nidx - qblk).astype(F32) * s256, MASK_BIAS)
        head_one = jnp.where(aux_row == hh, 1.0, 0.0)
        qaug.append(jnp.concatenate(
            [jnp.where(qt_row_head == hh, qt, 0.0), bias, head_one], axis=0).astype(BF16))
    qall = jnp.concatenate(qaug, axis=1)

    ncg = 4
    cg = [slice(c * blk, (c + 1) * blk) for c in range(ncg)]

    def softmax_c(st, m, c):
        mn = jnp.maximum(m, jnp.max(st, axis=0, keepdims=True))
        p_ref[:, cg[c]] = jnp.exp(st - mn).astype(BF16)
        return mn, jnp.exp(m - mn)

    def accumulate_c(n, alpha, c):
        off = pl.multiple_of(n * kt, kt)
        hh, a = c // 2, c % 2
        acc_ref[hh, :, a * blk:(a + 1) * blk] = (
            alpha * acc_ref[hh, :, a * blk:(a + 1) * blk]
            + _dot(vt_ref[hh, :, pl.ds(off, kt)], p_ref[:, cg[c]]))

    s_ref[...] = _dot(kaug_ref[0:kt, :], qall)
    p_ref[...] = jnp.zeros_like(p_ref)
    acc_ref[...] = jnp.zeros_like(acc_ref)
    m0 = tuple(jnp.full((1, blk), -jnp.inf, F32) for _ in range(ncg))
    a0 = tuple(jnp.ones((1, blk), F32) for _ in range(ncg))

    def body(i, carry):
        ms, alphas = carry
        off = pl.multiple_of((i + 1) * kt, kt)
        ktile = kaug_ref[pl.ds(off, kt), :]
        new_m, new_a, nxt = [], [], []
        for c in range(ncg):
            nxt.append(_dot(ktile, qall[:, cg[c]]))
            accumulate_c(jnp.maximum(i - 1, 0), alphas[c], c)
            mn, al = softmax_c(s_ref[:, cg[c]], ms[c], c)
            new_m.append(mn)
            new_a.append(al)
        for c in range(ncg):
            s_ref[:, cg[c]] = nxt[c]
        return tuple(new_m), tuple(new_a)

    ms, alphas = lax.fori_loop(0, j, body, (m0, a0))
    for c in range(ncg):
        accumulate_c(jnp.maximum(j - 1, 0), alphas[c], c)
    key_i = lax.broadcasted_iota(jnp.int32, (blk, blk), 0)
    qry_i = lax.broadcasted_iota(jnp.int32, (blk, blk), 1)
    causal = key_i <= qry_i
    off = pl.multiple_of(j * kt, kt)
    for c in range(ncg):
        hh, a = c // 2, c % 2
        parts = [jnp.where(causal, s_ref[0:blk, cg[c]], -jnp.inf)] if a == 0 else [
            s_ref[0:blk, cg[c]], jnp.where(causal, s_ref[blk:kt, cg[c]], -jnp.inf)]
        mn = ms[c]
        for st in parts:
            mn = jnp.maximum(mn, jnp.max(st, axis=0, keepdims=True))
        for r, st in enumerate(parts):
            p_ref[r * blk:(r + 1) * blk, cg[c]] = jnp.exp(st - mn).astype(BF16)
        nk = len(parts) * blk
        acc_ref[hh, :, a * blk:(a + 1) * blk] = (
            jnp.exp(ms[c] - mn) * acc_ref[hh, :, a * blk:(a + 1) * blk]
            + _dot(vt_ref[hh, :, pl.ds(off, nk)], p_ref[0:nk, cg[c]]))

    for a in range(2):
        ot = []
        for hh in range(2):
            acc = acc_ref[hh, :, a * blk:(a + 1) * blk]
            ot.append(acc[0:HEAD_DIM] * (1.0 / acc[HEAD_DIM:HEAD_DIM + 1]))
        o_ref[a * blk:(a + 1) * blk, :] = jnp.concatenate(ot, axis=0).T.astype(o_ref.dtype)


def _moba_tables(nblk):
    slopes = 2.0 ** (-8.0 * np.arange(1, N_MOBA_HEADS + 1) / N_MOBA_HEADS)
    npair = N_MOBA_HEADS // 2
    kaux = np.zeros((npair, MOBA_BLOCK, 2 * HEAD_DIM), np.float32)
    s256 = np.zeros((npair, 8, 2 * MOBA_BLOCK), np.float32)
    for p in range(npair):
        for hh in range(2):
            kaux[p, :, nblk + hh] = slopes[2 * p + hh] * np.arange(MOBA_BLOCK)
            s256[p, hh, :] = slopes[2 * p + hh] * MOBA_BLOCK
    return jnp.asarray(kaux, BF16), jnp.asarray(s256, F32)


def _moba(q, k, v, bsz):
    npair, n, hd2 = q.shape
    t = n // bsz
    blk = MOBA_BLOCK
    nblk = t // blk
    assert blk == 256 and t % (2 * blk) == 0 and nblk % 8 == 0 and nblk <= HEAD_DIM
    assert hd2 == 2 * HEAD_DIM
    tq = 2 * blk
    steps = t // tq
    kaux, s256 = _moba_tables(nblk)
    q_spec = pl.BlockSpec((None, tq, hd2), lambda bi, p, i: (p, bi * steps + i, 0))
    kv_spec = pl.BlockSpec((None, t, hd2), lambda bi, p, i: (p, bi, 0))
    return pl.pallas_call(
        functools.partial(_moba_kernel, nblk=nblk, blk=blk),
        grid=(bsz, npair, steps),
        in_specs=[q_spec, kv_spec, kv_spec,
                  pl.BlockSpec((1, blk, hd2), lambda bi, p, i: (p, 0, 0)),
                  pl.BlockSpec((1, 8, tq), lambda bi, p, i: (p, 0, 0))],
        out_specs=q_spec,
        out_shape=jax.ShapeDtypeStruct((npair, n, hd2), BF16),
        scratch_shapes=[pltpu.VMEM((t, 2 * hd2), BF16),
                        pltpu.VMEM((2, VT_ROWS, t), BF16),
                        pltpu.VMEM((nblk, hd2), F32),
                        pltpu.VMEM((2 * blk, 2 * tq), F32),
                        pltpu.VMEM((2 * blk, 2 * tq), BF16),
                        pltpu.VMEM((2, VT_ROWS, tq), F32)],
        compiler_params=_params("parallel", "parallel", "arbitrary"),
        name="moba",
    )(q, k, v, kaux, s256)


def _gla_kernel(qk_ref, v_ref, go_ref, gl_ref, wg_ref, bg_ref, gn_ref, o_ref, st_ref,
                *, sub, chunk):
    kw = N_GLA_HEADS * GLA_DK
    vw = N_GLA_HEADS * GLA_DV
    tt = qk_ref.shape[1]

    @pl.when(pl.program_id(1) == 0)
    def _reset():
        st_ref[...] = jnp.zeros_like(st_ref)

    r = lax.broadcasted_iota(jnp.int32, (sub, sub), 0)
    c = lax.broadcasted_iota(jnp.int32, (sub, sub), 1)
    same = (r >> 6) == (c >> 6)
    causal = same & (c <= r)
    ltri = jnp.where(causal, 1.0, 0.0).astype(BF16)
    eblk = jnp.where(same, 1.0, 0.0).astype(BF16)
    st_diag = ((lax.broadcasted_iota(jnp.int32, (vw, kw), 0) >> 6)
               == (lax.broadcasted_iota(jnp.int32, (vw, kw), 1) >> 5))
    grp = jnp.where((lax.broadcasted_iota(jnp.int32, (vw, vw), 0) >> 6)
                    == (lax.broadcasted_iota(jnp.int32, (vw, vw), 1) >> 6), 1.0, 0.0).astype(BF16)
    khead = lax.broadcasted_iota(jnp.int32, (sub, kw), 1) >> 5
    vhead = lax.broadcasted_iota(jnp.int32, (sub, vw), 1) >> 6

    for s in range(tt // sub):
        rows = slice(s * sub, (s + 1) * sub)
        z = _dot(gl_ref[0, rows, :], wg_ref[...]) + bg_ref[...]
        la = (jnp.minimum(z, 0.0) - jnp.log(1.0 + jnp.exp(-jnp.abs(z)))) * (1.0 / GLA_GATE_NORMALIZER)
        la_hi, la_lo = _split_bf16(la)
        b = _dot(ltri, la_hi) + _dot(ltri, la_lo)
        bl = _dot(eblk, la_hi) + _dot(eblk, la_lo)
        q = qk_ref[0, rows, 0:kw].astype(F32)
        k = qk_ref[0, rows, kw:2 * kw].astype(F32)
        v = v_ref[0, rows, :]
        qd = (q * (jnp.exp(b) * (GLA_DK ** -0.5))).astype(BF16)
        kd = (k * jnp.exp(-b)).astype(BF16)
        kend = (k * jnp.exp(bl - b)).astype(BF16)

        o = jnp.zeros((sub, vw), F32)
        for h in range(N_GLA_HEADS):
            a = _dot_nt(jnp.where(khead == h, qd, jnp.zeros_like(qd)), kd)
            a = jnp.where(causal, a, 0.0).astype(BF16)
            o = o + _dot(a, jnp.where(vhead == h, v, jnp.zeros_like(v)))

        inter = []
        for ci in range(sub // chunk):
            cr = slice(ci * chunk, (ci + 1) * chunk)
            st = st_ref[...]
            inter.append(_dot_nt(qd[cr], st.astype(BF16)))
            ut = _dot_tn(v[cr], kend[cr])
            decay = jnp.exp(bl[ci * chunk:ci * chunk + 1, :])
            st_ref[...] = st * decay + jnp.where(st_diag, ut, 0.0)
        o = o + jnp.concatenate(inter, axis=0)

        o2_hi, o2_lo = _split_bf16(o * o)
        ms = (_dot(o2_hi, grp) + _dot(o2_lo, grp)) * (1.0 / GLA_DV)
        go = go_ref[0, rows, :].astype(F32)
        y = (o * lax.rsqrt(ms + EPS)) * gn_ref[...] * (go * (1.0 / (1.0 + jnp.exp(-go))))
        o_ref[0, rows, :] = y.astype(o_ref.dtype)


def _gla(qk, v, go, gl, wg, bg, gn, l, tt):
    b, t, _ = qk.shape
    kw = N_GLA_HEADS * GLA_DK
    vw = N_GLA_HEADS * GLA_DV
    spec = lambda w: pl.BlockSpec((1, tt, w), lambda bi, i: (bi, i, 0))
    return pl.pallas_call(
        functools.partial(_gla_kernel, sub=4 * GLA_CHUNK, chunk=GLA_CHUNK),
        grid=(b, t // tt),
        in_specs=[spec(2 * kw), spec(vw), spec(vw), spec(LANES),
                  _layer(wg, l), _layer(bg, l), _layer(gn, l)],
        out_specs=spec(vw),
        out_shape=jax.ShapeDtypeStruct((b, t, vw), BF16),
        scratch_shapes=[pltpu.VMEM((vw, kw), F32)],
        compiler_params=_params("parallel", "arbitrary"),
        name="gla",
    )(qk, v, go, gl, wg, bg, gn)


def _mem_kv_kernel(mem_ref, g_ref, w_ref, k_ref, v_ref):
    h = _rms(mem_ref[0], g_ref[0]).astype(BF16)
    kv = _dot(h, w_ref[0])
    wd = k_ref.shape[-1]
    k_ref[0, 0] = kv[:, :wd].astype(BF16)
    v_ref[0, 0] = kv[:, wd:].astype(BF16)


def _mem_kv(mem, g, wkv):
    b, m, d = mem.shape
    nl = wkv.shape[0]
    wd = wkv.shape[-1] // 2
    out = jax.ShapeDtypeStruct((nl, b, m, wd), BF16)
    ospec = pl.BlockSpec((1, 1, m, wd), lambda l, bi: (l, bi, 0, 0))
    return pl.pallas_call(
        _mem_kv_kernel,
        grid=(nl, b),
        in_specs=[pl.BlockSpec((1, m, d), lambda l, bi: (bi, 0, 0)),
                  pl.BlockSpec((1, 1, d), lambda l, bi: (l, 0, 0)),
                  pl.BlockSpec((1, d, 2 * wd), lambda l, bi: (l, 0, 0))],
        out_specs=[ospec, ospec],
        out_shape=[out, out],
        compiler_params=_params("parallel", "parallel"),
        name="mem_kv",
    )(mem, g, wkv)


def _mix_xattn_kernel(x_ref, c3_ref, c3h_ref, ym_ref, yg_ref, wout_ref, cw_ref, cb_ref,
                      gx_ref, wq_ref, mk_ref, mv_ref, wo_ref, o_ref, *, tiles_per_seq):
    tm = x_ref.shape[0]
    cc = cw_ref.shape[-1]
    nh = N_XATTN_HEADS
    first = (pl.program_id(0) % tiles_per_seq) == 0

    c3 = c3_ref[...].astype(F32)
    u = c3[:, cc:2 * cc] * c3[:, 2 * cc:3 * cc]
    halo = c3h_ref[...].astype(F32)
    uh = halo[:, cc:2 * cc] * halo[:, 2 * cc:3 * cc]
    uh = jnp.where(first, 0.0, uh)
    last1 = uh[BF16_SUBLANES - 1:BF16_SUBLANES, :]
    last2 = uh[BF16_SUBLANES - 2:BF16_SUBLANES - 1, :]
    row = lax.broadcasted_iota(jnp.int32, (tm, cc), 0)
    u1 = jnp.where(row == 0, last1, pltpu.roll(u, 1, 0))
    u2 = jnp.where(row == 0, last2, jnp.where(row == 1, last1, pltpu.roll(u, 2, 0)))
    cw = cw_ref[...]
    yc = c3[:, 0:cc] * (cw[0:1] * u2 + cw[1:2] * u1 + cw[2:3] * u + cb_ref[...])

    ym = jnp.concatenate([ym_ref[p] for p in range(ym_ref.shape[0])], axis=1)
    mw = ym.shape[-1]
    x1 = (x_ref[...]
          + _dot(yc.astype(BF16), wout_ref[0:cc, :])
          + _dot(ym, wout_ref[cc:cc + mw, :])
          + _dot(yg_ref[...], wout_ref[cc + mw:, :]))

    h = _rms(x1, gx_ref[...]).astype(BF16)
    q = _dot(h, wq_ref[...]).astype(BF16)
    k = mk_ref[...]
    v = mv_ref[...]
    qhead = lax.broadcasted_iota(jnp.int32, q.shape, 1) >> 6
    vhead = lax.broadcasted_iota(jnp.int32, v.shape, 1) >> 6
    o = jnp.zeros(q.shape, F32)
    for hd in range(nh):
        s = _dot_nt(jnp.where(qhead == hd, q, jnp.zeros_like(q)), k)
        p = jnp.exp(s - jnp.max(s, axis=-1, keepdims=True))
        l = jnp.sum(p, axis=-1, keepdims=True)
        pv = _dot(p.astype(BF16), jnp.where(vhead == hd, v, jnp.zeros_like(v)))
        o = o + pv * (1.0 / l)
    o_ref[...] = x1 + _dot(o.astype(BF16), wo_ref[...])


def _mix_xattn(x, c3, ym, yg, wout, cw, cb, gx, wq, memk, memv, wo, l, *, tm, seq):
    n, d = x.shape
    tiles_per_seq = seq // tm
    hb = tm // BF16_SUBLANES
    row = lambda w: pl.BlockSpec((tm, w), lambda i: (i, 0))
    mem_spec = pl.BlockSpec((None, None) + memk.shape[2:], lambda i: (l, i // tiles_per_seq, 0, 0))
    return pl.pallas_call(
        functools.partial(_mix_xattn_kernel, tiles_per_seq=tiles_per_seq),
        grid=(n // tm,),
        in_specs=[row(d), row(c3.shape[1]),
                  pl.BlockSpec((BF16_SUBLANES, c3.shape[1]),
                               lambda i: (jnp.maximum(i * hb - 1, 0), 0)),
                  pl.BlockSpec((ym.shape[0], tm, ym.shape[2]), lambda i: (0, i, 0)),
                  row(yg.shape[1]),
                  _layer(wout, l), _layer(cw, l), _layer(cb, l), _layer(gx, l), _layer(wq, l),
                  mem_spec, mem_spec, _layer(wo, l)],
        out_specs=row(d),
        out_shape=jax.ShapeDtypeStruct((n, d), F32),
        compiler_params=_params("parallel"),
        name="mix_xattn",
    )(x, c3, c3, ym, yg, wout, cw, cb, gx, wq, memk, memv, wo)


def _ffn_kernel(x_ref, g_ref, wa_ref, wg_ref, cwa_ref, cwg_ref, cba_ref, cbg_ref, wd_ref,
                gf_ref, o_ref, hext_ref, acc_ref, ua_ref, ug_ref, *, tiles_per_seq, final):
    tm = x_ref.shape[0]
    halo = BF16_SUBLANES
    tn = ua_ref.shape[1]
    nc = wa_ref.shape[1] // tn
    first = (pl.program_id(0) % tiles_per_seq) == 0

    def cols(j):
        return pl.ds(pl.multiple_of(j * tn, tn), tn)

    @pl.when(first)
    def _zero_halo():
        hext_ref[0:halo, :] = jnp.zeros((halo, hext_ref.shape[1]), BF16)

    @pl.when(jnp.logical_not(first))
    def _carry_halo():
        hext_ref[0:halo, :] = hext_ref[tm:tm + halo, :]

    x = x_ref[...]
    hext_ref[halo:, :] = _rms(x, g_ref[...]).astype(BF16)
    acc_ref[...] = jnp.zeros_like(acc_ref)

    def up(j):
        hext = hext_ref[...]
        return _dot(hext, wa_ref[:, cols(j)]), _dot(hext, wg_ref[:, cols(j)])

    def conv(u, cw, cb):
        y = cw[0:1] * pltpu.roll(u, 2, 0) + cw[1:2] * pltpu.roll(u, 1, 0) + cw[2:3] * u + cb
        return y[F32_SUBLANES:]

    def down(j):
        src = slice(halo - F32_SUBLANES, halo + tm)
        a = conv(ua_ref[src, :], cwa_ref[:, cols(j)], cba_ref[:, cols(j)])
        g = conv(ug_ref[src, :], cwg_ref[:, cols(j)], cbg_ref[:, cols(j)])
        hid = ((a * (1.0 / (1.0 + jnp.exp(-a)))) * g).astype(BF16)
        acc_ref[...] += _dot(hid, wd_ref[cols(j), :])

    ua_ref[...], ug_ref[...] = up(0)

    def body(j, carry):
        ua, ug = up(j + 1)
        down(j)
        ua_ref[...] = ua
        ug_ref[...] = ug
        return carry

    lax.fori_loop(0, nc - 1, body, 0)
    down(nc - 1)
    y = x + acc_ref[...]
    if final:
        y = _rms(y, gf_ref[...])
    o_ref[...] = y


def _ffn(x, g, wa, wg, cwa, cwg, cba, cbg, wd, gf, l, *, tm, tn, seq, final):
    n, d = x.shape
    return pl.pallas_call(
        functools.partial(_ffn_kernel, tiles_per_seq=seq // tm, final=final),
        grid=(n // tm,),
        in_specs=[pl.BlockSpec((tm, d), lambda i: (i, 0)), _layer(g, l),
                  _layer(wa, l), _layer(wg, l), _layer(cwa, l), _layer(cwg, l),
                  _layer(cba, l), _layer(cbg, l), _layer(wd, l), _resident(gf.shape)],
        out_specs=pl.BlockSpec((tm, d), lambda i: (i, 0)),
        out_shape=jax.ShapeDtypeStruct((n, d), F32),
        scratch_shapes=[pltpu.VMEM((tm + BF16_SUBLANES, d), BF16),
                        pltpu.VMEM((tm, d), F32),
                        pltpu.VMEM((tm + BF16_SUBLANES, tn), F32),
                        pltpu.VMEM((tm + BF16_SUBLANES, tn), F32)],
        compiler_params=_params("arbitrary"),
        name="ffn",
    )(x, g, wa, wg, cwa, cwg, cba, cbg, wd, gf)


def _pad_cols(w, pad_to):
    return jnp.pad(w, [(0, 0)] * (w.ndim - 1) + [(0, pad_to - w.shape[-1])])


def kernel(x, mem, norm_mix_g, w_in, sc_conv_w, sc_conv_b, gla_w_gate, gla_b_gate, gla_norm_g,
           w_out, norm_xattn_g, norm_mem_g, xattn_wq, xattn_wkv, xattn_wo, norm_ffn_g,
           ffn_w_up, ffn_conv_w, ffn_conv_b, ffn_w_down, final_norm_g):
    bsz, seq, d = x.shape
    nl = w_in.shape[0]
    n = bsz * seq
    conv_ch = sc_conv_w.shape[-1]
    moba_w = N_MOBA_HEADS * HEAD_DIM
    gk_w = N_GLA_HEADS * GLA_DK
    gv_w = N_GLA_HEADS * GLA_DV
    d_ff = ffn_w_down.shape[1]
    ff_pad = -(-d_ff // FFN_CHUNK) * FFN_CHUNK
    tm = TOKEN_TILE

    widths = (3 * conv_ch, moba_w, moba_w, moba_w, 2 * gk_w, gv_w, gv_w, LANES)
    pair_major = (False, True, True, True, False, False, False, False)
    in_scales = (1.0, HEAD_DIM ** -0.5, 1.0, 1.0, 1.0, 1.0, 1.0)
    lr0 = sum(widths[:-1])
    w_lr = _pad_cols(w_in[..., lr0:].astype(BF16), LANES)
    wg_p = jnp.pad(gla_w_gate.astype(BF16), ((0, 0), (0, LANES - GLA_GATE_RANK), (0, 0)))
    gn_t = jnp.tile(gla_norm_g, (1, N_GLA_HEADS))
    w_out_b = w_out.astype(BF16)
    wq_b = (xattn_wq * (HEAD_DIM ** -0.5)).astype(BF16)
    wkv_b = xattn_wkv.astype(BF16)
    wo_b = xattn_wo.astype(BF16)
    wa = _pad_cols(ffn_w_up[..., :d_ff].astype(BF16), ff_pad)
    wgt = _pad_cols(ffn_w_up[..., d_ff:].astype(BF16), ff_pad)
    cwa = _pad_cols(ffn_conv_w[..., :d_ff], ff_pad)
    cwg = _pad_cols(ffn_conv_w[..., d_ff:], ff_pad)
    cba = _pad_cols(ffn_conv_b[:, None, :d_ff], ff_pad)
    cbg = _pad_cols(ffn_conv_b[:, None, d_ff:], ff_pad)
    wd = jnp.pad(ffn_w_down.astype(BF16), ((0, 0), (0, ff_pad - d_ff), (0, 0)))

    memk, memv = _mem_kv(mem, norm_mem_g[:, None, :], wkv_b)

    vec = lambda a: a[:, None, :]
    norm_mix_v, norm_xattn_v, norm_ffn_v = vec(norm_mix_g), vec(norm_xattn_g), vec(norm_ffn_g)
    gla_bg_v, gn_v, sc_conv_b_v = vec(gla_b_gate), vec(gn_t), vec(sc_conv_b)

    xf = x.reshape(n, d)
    for l in range(nl):
        c3, mq, mk, mv, gqk, gv, go, gl = _norm_inproj(
            xf, norm_mix_v, w_in, w_lr, l, widths, pair_major, in_scales, tm)
        r3 = lambda a: a.reshape(bsz, seq, a.shape[-1])
        ym = _moba(mq, mk, mv, bsz)
        yg = _gla(r3(gqk), r3(gv), r3(go), r3(gl), wg_p, gla_bg_v, gn_v, l,
                  tt=GLA_TILE).reshape(n, gv_w)
        xf = _mix_xattn(xf, c3, ym, yg, w_out_b, sc_conv_w, sc_conv_b_v, norm_xattn_v, wq_b,
                        memk, memv, wo_b, l, tm=tm, seq=seq)
        xf = _ffn(xf, norm_ffn_v, wa, wgt, cwa, cwg, cba, cbg, wd, final_norm_g[None, :], l,
                  tm=tm, tn=FFN_CHUNK, seq=seq, final=(l == nl - 1))
    return xf.reshape(bsz, seq, d)
```

```python
import functools

import numpy as np
import jax
import jax.numpy as jnp
from jax import lax
from jax.experimental import pallas as pl
from jax.experimental.pallas import tpu as pltpu

F32 = jnp.float32
BF16 = jnp.bfloat16

EPS = 1e-6
HEAD_DIM = 64
N_MOBA_HEADS = 8
MOBA_BLOCK = 256
MOBA_TOPK = 3
N_GLA_HEADS = 4
GLA_DK = 32
GLA_DV = 64
GLA_CHUNK = 64
GLA_GATE_RANK = 16
GLA_GATE_NORMALIZER = 16.0
CONV_K = 3
N_XATTN_HEADS = 4

LANES = 128
F32_SUBLANES = 8
BF16_SUBLANES = 16
VMEM_LIMIT = 56 * 1024 * 1024
TOKEN_TILE = 1024
FFN_CHUNK = 256
GLA_TILE = 1024
WT_CHUNK = 256

VT_ROWS = HEAD_DIM + 16
MASK_BIAS = -(2.0 ** 60)

_NT = (((1,), (1,)), ((), ()))
_TN = (((0,), (0,)), ((), ()))


def _dot(a, b):
    return jnp.dot(a, b, preferred_element_type=F32)


def _dot_nt(a, b):
    return lax.dot_general(a, b, _NT, preferred_element_type=F32)


def _dot_tn(a, b):
    return lax.dot_general(a, b, _TN, preferred_element_type=F32)


def _split_bf16(a):
    hi = a.astype(BF16)
    lo = (a - hi.astype(F32)).astype(BF16)
    return hi, lo


def _rms(x, g):
    r = lax.rsqrt(jnp.mean(x * x, axis=-1, keepdims=True) + EPS)
    return (x * r) * g


def _params(*sem):
    return pltpu.CompilerParams(dimension_semantics=sem, vmem_limit_bytes=VMEM_LIMIT)


def _resident(shape):
    nd = len(shape)
    return pl.BlockSpec(shape, lambda *_: (0,) * nd, pipeline_mode=pl.Buffered(1))


def _layer(arr, l):
    nd = arr.ndim
    return pl.BlockSpec((None,) + arr.shape[1:], lambda *_: (l,) + (0,) * (nd - 1),
                        pipeline_mode=pl.Buffered(1))


def _norm_inproj_kernel(x_ref, g_ref, wt_ref, *refs, scales, tail):
    out_refs, wb_ref = refs[:-1], refs[-1]
    in_w = wt_ref.shape[0]

    @pl.when(pl.program_id(0) == 0)
    def _prepare():
        off = 0
        for o_ref, scale in zip(out_refs[:-1], scales):
            wd = o_ref.shape[0] * o_ref.shape[2] if len(o_ref.shape) == 3 else o_ref.shape[-1]
            for c0 in range(off, off + wd, WT_CHUNK):
                blk = wt_ref[c0:c0 + WT_CHUNK, :]
                if scale != 1.0:
                    blk = blk * scale
                wb_ref[:, c0:c0 + WT_CHUNK] = blk.T.astype(BF16)
            off += wd
        wb_ref[:, off:off + tail] = wt_ref[in_w - tail:in_w, :].T.astype(BF16)

    h = _rms(x_ref[...], g_ref[...]).astype(BF16)
    off = 0
    for o_ref in out_refs:
        pair_major = len(o_ref.shape) == 3
        wd = o_ref.shape[0] * o_ref.shape[2] if pair_major else o_ref.shape[-1]
        res = _dot(h, wb_ref[:, off:off + wd]).astype(o_ref.dtype)
        if pair_major:
            for p in range(o_ref.shape[0]):
                o_ref[p] = res[:, p * o_ref.shape[2]:(p + 1) * o_ref.shape[2]]
        else:
            o_ref[...] = res
        off += wd


def _norm_inproj(x, g, wt, l, widths, pair_major, scales, tm):
    n, d = x.shape
    assert all(wd % WT_CHUNK == 0 for wd in widths[:-1]) and sum(widths[:-1]) <= wt.shape[1]
    specs, shapes = [], []
    for wd, pm in zip(widths, pair_major):
        if pm:
            specs.append(pl.BlockSpec((wd // LANES, tm, LANES), lambda i: (0, i, 0)))
            shapes.append(jax.ShapeDtypeStruct((wd // LANES, n, LANES), BF16))
        else:
            specs.append(pl.BlockSpec((tm, wd), lambda i: (i, 0)))
            shapes.append(jax.ShapeDtypeStruct((n, wd), BF16))
    return pl.pallas_call(
        functools.partial(_norm_inproj_kernel, scales=scales, tail=widths[-1]),
        grid=(n // tm,),
        in_specs=[pl.BlockSpec((tm, d), lambda i: (i, 0)), _layer(g, l), _layer(wt, l)],
        out_specs=specs,
        out_shape=shapes,
        scratch_shapes=[pltpu.VMEM((d, sum(widths)), BF16)],
        compiler_params=_params("arbitrary"),
        name="norm_inproj",
    )(x, g, wt)


def _moba_kernel(q_ref, k_ref, v_ref, kaux_ref, s256_ref, o_ref,
                 kaug_ref, vt_ref, kmean_ref, s_ref, p_ref, acc_ref, *, nblk, blk):
    j = pl.program_id(2)
    hd2 = 2 * HEAD_DIM
    tq = 2 * blk
    kt = 2 * blk

    @pl.when(j == 0)
    def _prepare():
        lane = lax.broadcasted_iota(jnp.int32, (blk, hd2), 1)
        aux = kaux_ref[0].astype(F32)
        ones = jnp.ones((VT_ROWS - HEAD_DIM, blk), BF16)
        for n in range(nblk):
            rows = slice(n * blk, (n + 1) * blk)
            kb = k_ref[rows, :]
            kaug_ref[rows, 0:hd2] = kb
            kaug_ref[rows, hd2:2 * hd2] = (aux + jnp.where(lane == n, 1.0, 0.0)).astype(BF16)
            vt = v_ref[rows, :].astype(F32).T.astype(BF16)
            for hh in range(2):
                vt_ref[hh, 0:HEAD_DIM, rows] = vt[hh * HEAD_DIM:(hh + 1) * HEAD_DIM]
                vt_ref[hh, HEAD_DIM:VT_ROWS, rows] = ones
            kmean_ref[n:n + 1, :] = jnp.sum(kb.astype(F32), axis=0, keepdims=True) * (1.0 / blk)

    qt = q_ref[...].astype(F32).T
    qt_b = qt.astype(BF16)
    km = kmean_ref[...]
    km_lane_head = lax.broadcasted_iota(jnp.int32, (nblk, hd2), 1) >> 6
    qt_row_head = lax.broadcasted_iota(jnp.int32, (hd2, tq), 0) >> 6
    nidx = lax.broadcasted_iota(jnp.int32, (nblk, tq), 0)
    qblk = 2 * j + (lax.broadcasted_iota(jnp.int32, (nblk, tq), 1) >> 8)
    aux_row = lax.broadcasted_iota(jnp.int32, (hd2 - nblk, tq), 0)
    past = nidx < qblk

    qaug = []
    for hh in range(2):
        kmh = jnp.where(km_lane_head == hh, km, 0.0)
        kh, kl = _split_bf16(kmh)
        gate = _dot(kh, qt_b) + _dot(kl, qt_b)
        g = jnp.where(past, gate, -jnp.inf)
        rank = jnp.zeros((nblk, tq), F32)
        for m in range(nblk):
            gm = g[m:m + 1, :]
            rank = rank + jnp.where(m < nidx, jnp.where(gm >= g, 1.0, 0.0), jnp.where(gm > g, 1.0, 0.0))
        sel = (past & (rank < MOBA_TOPK)) | (nidx == qblk)
        s256 = s256_ref[0, hh:hh + 1, :]
        bias = jnp.where(sel, (nidx - qblk).astype(F32) * s256, MASK_BIAS)
        head_one = jnp.where(aux_row == hh, 1.0, 0.0)
        qaug.append(jnp.concatenate(
            [jnp.where(qt_row_head == hh, qt, 0.0), bias, head_one], axis=0).astype(BF16))
    qall = jnp.concatenate(qaug, axis=1)

    ncg = 4
    cg = [slice(c * blk, (c + 1) * blk) for c in range(ncg)]

    def softmax_c(st, m, c):
        mn = jnp.maximum(m, jnp.max(st, axis=0, keepdims=True))
        p_ref[:, cg[c]] = jnp.exp(st - mn).astype(BF16)
        return mn, jnp.exp(m - mn)

    def accumulate_c(n, alpha, c):
        off = pl.multiple_of(n * kt, kt)
        hh, a = c // 2, c % 2
        acc_ref[hh, :, a * blk:(a + 1) * blk] = (
            alpha * acc_ref[hh, :, a * blk:(a + 1) * blk]
            + _dot(vt_ref[hh, :, pl.ds(off, kt)], p_ref[:, cg[c]]))

    s_ref[...] = _dot(kaug_ref[0:kt, :], qall)
    p_ref[...] = jnp.zeros_like(p_ref)
    acc_ref[...] = jnp.zeros_like(acc_ref)
    m0 = tuple(jnp.full((1, blk), -jnp.inf, F32) for _ in range(ncg))
    a0 = tuple(jnp.ones((1, blk), F32) for _ in range(ncg))

    def body(i, carry):
        ms, alphas = carry
        off = pl.multiple_of((i + 1) * kt, kt)
        ktile = kaug_ref[pl.ds(off, kt), :]
        new_m, new_a, nxt = [], [], []
        for c in range(ncg):
            nxt.append(_dot(ktile, qall[:, cg[c]]))
            accumulate_c(jnp.maximum(i - 1, 0), alphas[c], c)
            mn, al = softmax_c(s_ref[:, cg[c]], ms[c], c)
            new_m.append(mn)
            new_a.append(al)
        for c in range(ncg):
            s_ref[:, cg[c]] = nxt[c]
        return tuple(new_m), tuple(new_a)

    ms, alphas = lax.fori_loop(0, j, body, (m0, a0))
    for c in range(ncg):
        accumulate_c(jnp.maximum(j - 1, 0), alphas[c], c)
    key_i = lax.broadcasted_iota(jnp.int32, (blk, blk), 0)
    qry_i = lax.broadcasted_iota(jnp.int32, (blk, blk), 1)
    causal = key_i <= qry_i
    off = pl.multiple_of(j * kt, kt)
    for c in range(ncg):
        hh, a = c // 2, c % 2
        parts = [jnp.where(causal, s_ref[0:blk, cg[c]], -jnp.inf)] if a == 0 else [
            s_ref[0:blk, cg[c]], jnp.where(causal, s_ref[blk:kt, cg[c]], -jnp.inf)]
        mn = ms[c]
        for st in parts:
            mn = jnp.maximum(mn, jnp.max(st, axis=0, keepdims=True))
        for r, st in enumerate(parts):
            p_ref[r * blk:(r + 1) * blk, cg[c]] = jnp.exp(st - mn).astype(BF16)
        nk = len(parts) * blk
        acc_ref[hh, :, a * blk:(a + 1) * blk] = (
            jnp.exp(ms[c] - mn) * acc_ref[hh, :, a * blk:(a + 1) * blk]
            + _dot(vt_ref[hh, :, pl.ds(off, nk)], p_ref[0:nk, cg[c]]))

    for a in range(2):
        ot = []
        for hh in range(2):
            acc = acc_ref[hh, :, a * blk:(a + 1) * blk]
            ot.append(acc[0:HEAD_DIM] * (1.0 / acc[HEAD_DIM:HEAD_DIM + 1]))
        o_ref[a * blk:(a + 1) * blk, :] = jnp.concatenate(ot, axis=0).T.astype(o_ref.dtype)


def _moba_tables(nblk):
    slopes = 2.0 ** (-8.0 * np.arange(1, N_MOBA_HEADS + 1) / N_MOBA_HEADS)
    npair = N_MOBA_HEADS // 2
    kaux = np.zeros((npair, MOBA_BLOCK, 2 * HEAD_DIM), np.float32)
    s256 = np.zeros((npair, 8, 2 * MOBA_BLOCK), np.float32)
    for p in range(npair):
        for hh in range(2):
            kaux[p, :, nblk + hh] = slopes[2 * p + hh] * np.arange(MOBA_BLOCK)
            s256[p, hh, :] = slopes[2 * p + hh] * MOBA_BLOCK
    return jnp.asarray(kaux, BF16), jnp.asarray(s256, F32)


def _moba(q, k, v, bsz):
    npair, n, hd2 = q.shape
    t = n // bsz
    blk = MOBA_BLOCK
    nblk = t // blk
    assert blk == 256 and t % (2 * blk) == 0 and nblk % 8 == 0 and nblk <= HEAD_DIM
    assert hd2 == 2 * HEAD_DIM
    tq = 2 * blk
    steps = t // tq
    kaux, s256 = _moba_tables(nblk)
    q_spec = pl.BlockSpec((None, tq, hd2), lambda bi, p, i: (p, bi * steps + i, 0))
    kv_spec = pl.BlockSpec((None, t, hd2), lambda bi, p, i: (p, bi, 0))
    return pl.pallas_call(
        functools.partial(_moba_kernel, nblk=nblk, blk=blk),
        grid=(bsz, npair, steps),
        in_specs=[q_spec, kv_spec, kv_spec,
                  pl.BlockSpec((1, blk, hd2), lambda bi, p, i: (p, 0, 0)),
                  pl.BlockSpec((1, 8, tq), lambda bi, p, i: (p, 0, 0))],
        out_specs=q_spec,
        out_shape=jax.ShapeDtypeStruct((npair, n, hd2), BF16),
        scratch_shapes=[pltpu.VMEM((t, 2 * hd2), BF16),
                        pltpu.VMEM((2, VT_ROWS, t), BF16),
                        pltpu.VMEM((nblk, hd2), F32),
                        pltpu.VMEM((2 * blk, 2 * tq), F32),
                        pltpu.VMEM((2 * blk, 2 * tq), BF16),
                        pltpu.VMEM((2, VT_ROWS, tq), F32)],
        compiler_params=_params("parallel", "parallel", "arbitrary"),
        name="moba",
    )(q, k, v, kaux, s256)


def _gla_kernel(qk_ref, v_ref, go_ref, gl_ref, wg_ref, bg_ref, gn_ref, o_ref, st_ref,
                *, sub, chunk):
    kw = N_GLA_HEADS * GLA_DK
    vw = N_GLA_HEADS * GLA_DV
    tt = qk_ref.shape[1]

    @pl.when(pl.program_id(1) == 0)
    def _reset():
        st_ref[...] = jnp.zeros_like(st_ref)

    r = lax.broadcasted_iota(jnp.int32, (sub, sub), 0)
    c = lax.broadcasted_iota(jnp.int32, (sub, sub), 1)
    same = (r >> 6) == (c >> 6)
    causal = same & (c <= r)
    ltri = jnp.where(causal, 1.0, 0.0).astype(BF16)
    st_diag = ((lax.broadcasted_iota(jnp.int32, (vw, kw), 0) >> 6)
               == (lax.broadcasted_iota(jnp.int32, (vw, kw), 1) >> 5))
    grp = jnp.where((lax.broadcasted_iota(jnp.int32, (vw, vw), 0) >> 6)
                    == (lax.broadcasted_iota(jnp.int32, (vw, vw), 1) >> 6), 1.0, 0.0).astype(BF16)
    khead = lax.broadcasted_iota(jnp.int32, (sub, kw), 1) >> 5
    vhead = lax.broadcasted_iota(jnp.int32, (sub, vw), 1) >> 6

    for s in range(tt // sub):
        rows = slice(s * sub, (s + 1) * sub)
        z = _dot(gl_ref[0, rows, :], wg_ref[...]) + bg_ref[...]
        la = (jnp.minimum(z, 0.0) - jnp.log(1.0 + jnp.exp(-jnp.abs(z)))) * (1.0 / GLA_GATE_NORMALIZER)
        la_hi, la_lo = _split_bf16(la)
        b = _dot(ltri, la_hi) + _dot(ltri, la_lo)
        tot = [b[(ci + 1) * chunk - 1:(ci + 1) * chunk, :] for ci in range(sub // chunk)]
        bl = jnp.concatenate([jnp.broadcast_to(t, (chunk, kw)) for t in tot], axis=0)
        q = qk_ref[0, rows, 0:kw].astype(F32)
        k = qk_ref[0, rows, kw:2 * kw].astype(F32)
        v = v_ref[0, rows, :]
        qd = (q * (jnp.exp(b) * (GLA_DK ** -0.5))).astype(BF16)
        kd = (k * jnp.exp(-b)).astype(BF16)
        kend = (k * jnp.exp(bl - b)).astype(BF16)

        o = jnp.zeros((sub, vw), F32)
        for h in range(N_GLA_HEADS):
            a = _dot_nt(jnp.where(khead == h, qd, jnp.zeros_like(qd)), kd)
            a = jnp.where(causal, a, 0.0).astype(BF16)
            o = o + _dot(a, jnp.where(vhead == h, v, jnp.zeros_like(v)))

        inter = []
        for ci in range(sub // chunk):
            cr = slice(ci * chunk, (ci + 1) * chunk)
            st = st_ref[...]
            inter.append(_dot_nt(qd[cr], st.astype(BF16)))
            ut = _dot_tn(v[cr], kend[cr])
            st_ref[...] = st * jnp.exp(tot[ci]) + jnp.where(st_diag, ut, 0.0)
        o = o + jnp.concatenate(inter, axis=0)

        o2_hi, o2_lo = _split_bf16(o * o)
        ms = (_dot(o2_hi, grp) + _dot(o2_lo, grp)) * (1.0 / GLA_DV)
        go = go_ref[0, rows, :].astype(F32)
        y = (o * lax.rsqrt(ms + EPS)) * gn_ref[...] * (go * (1.0 / (1.0 + jnp.exp(-go))))
        o_ref[0, rows, :] = y.astype(o_ref.dtype)


def _gla(qk, v, go, gl, wg, bg, gn, l, tt):
    b, t, _ = qk.shape
    kw = N_GLA_HEADS * GLA_DK
    vw = N_GLA_HEADS * GLA_DV
    spec = lambda w: pl.BlockSpec((1, tt, w), lambda bi, i: (bi, i, 0))
    return pl.pallas_call(
        functools.partial(_gla_kernel, sub=4 * GLA_CHUNK, chunk=GLA_CHUNK),
        grid=(b, t // tt),
        in_specs=[spec(2 * kw), spec(vw), spec(vw), spec(LANES),
                  _layer(wg, l), _layer(bg, l), _layer(gn, l)],
        out_specs=spec(vw),
        out_shape=jax.ShapeDtypeStruct((b, t, vw), BF16),
        scratch_shapes=[pltpu.VMEM((vw, kw), F32)],
        compiler_params=_params("parallel", "arbitrary"),
        name="gla",
    )(qk, v, go, gl, wg, bg, gn)


def _mem_kv_kernel(mem_ref, g_ref, w_ref, k_ref, v_ref):
    h = _rms(mem_ref[0], g_ref[0]).astype(BF16)
    kv = _dot(h, w_ref[0])
    wd = k_ref.shape[-1]
    k_ref[0, 0] = kv[:, :wd].astype(BF16)
    v_ref[0, 0] = kv[:, wd:].astype(BF16)


def _mem_kv(mem, g, wkv):
    b, m, d = mem.shape
    nl = wkv.shape[0]
    wd = wkv.shape[-1] // 2
    out = jax.ShapeDtypeStruct((nl, b, m, wd), BF16)
    ospec = pl.BlockSpec((1, 1, m, wd), lambda l, bi: (l, bi, 0, 0))
    return pl.pallas_call(
        _mem_kv_kernel,
        grid=(nl, b),
        in_specs=[pl.BlockSpec((1, m, d), lambda l, bi: (bi, 0, 0)),
                  pl.BlockSpec((1, 1, d), lambda l, bi: (l, 0, 0)),
                  pl.BlockSpec((1, d, 2 * wd), lambda l, bi: (l, 0, 0))],
        out_specs=[ospec, ospec],
        out_shape=[out, out],
        compiler_params=_params("parallel", "parallel"),
        name="mem_kv",
    )(mem, g, wkv)


def _mix_xattn_kernel(x_ref, c3_ref, c3h_ref, ym_ref, yg_ref, wout_ref, cw_ref, cb_ref,
                      gx_ref, wq_ref, mk_ref, mv_ref, wo_ref, o_ref, *, tiles_per_seq):
    tm = x_ref.shape[0]
    cc = cw_ref.shape[-1]
    nh = N_XATTN_HEADS
    first = (pl.program_id(0) % tiles_per_seq) == 0

    c3 = c3_ref[...].astype(F32)
    u = c3[:, cc:2 * cc] * c3[:, 2 * cc:3 * cc]
    halo = c3h_ref[...].astype(F32)
    uh = halo[:, cc:2 * cc] * halo[:, 2 * cc:3 * cc]
    uh = jnp.where(first, 0.0, uh)
    last1 = uh[BF16_SUBLANES - 1:BF16_SUBLANES, :]
    last2 = uh[BF16_SUBLANES - 2:BF16_SUBLANES - 1, :]
    row = lax.broadcasted_iota(jnp.int32, (tm, cc), 0)
    u1 = jnp.where(row == 0, last1, pltpu.roll(u, 1, 0))
    u2 = jnp.where(row == 0, last2, jnp.where(row == 1, last1, pltpu.roll(u, 2, 0)))
    cw = cw_ref[...]
    yc = c3[:, 0:cc] * (cw[0:1] * u2 + cw[1:2] * u1 + cw[2:3] * u + cb_ref[...])

    ym = jnp.concatenate([ym_ref[p] for p in range(ym_ref.shape[0])], axis=1)
    mw = ym.shape[-1]
    x1 = (x_ref[...]
          + _dot(yc.astype(BF16), wout_ref[0:cc, :])
          + _dot(ym, wout_ref[cc:cc + mw, :])
          + _dot(yg_ref[...], wout_ref[cc + mw:, :]))

    h = _rms(x1, gx_ref[...]).astype(BF16)
    q = _dot(h, wq_ref[...]).astype(BF16)
    k = mk_ref[...]
    v = mv_ref[...]
    qhead = lax.broadcasted_iota(jnp.int32, q.shape, 1) >> 6
    vhead = lax.broadcasted_iota(jnp.int32, v.shape, 1) >> 6
    o = jnp.zeros(q.shape, F32)
    for hd in range(nh):
        s = _dot_nt(jnp.where(qhead == hd, q, jnp.zeros_like(q)), k)
        p = jnp.exp(s - jnp.max(s, axis=-1, keepdims=True))
        l = jnp.sum(p, axis=-1, keepdims=True)
        pv = _dot(p.astype(BF16), jnp.where(vhead == hd, v, jnp.zeros_like(v)))
        o = o + pv * (1.0 / l)
    o_ref[...] = x1 + _dot(o.astype(BF16), wo_ref[...])


def _mix_xattn(x, c3, ym, yg, wout, cw, cb, gx, wq, memk, memv, wo, l, *, tm, seq):
    n, d = x.shape
    tiles_per_seq = seq // tm
    hb = tm // BF16_SUBLANES
    row = lambda w: pl.BlockSpec((tm, w), lambda i: (i, 0))
    mem_spec = pl.BlockSpec((None, None) + memk.shape[2:], lambda i: (l, i // tiles_per_seq, 0, 0))
    return pl.pallas_call(
        functools.partial(_mix_xattn_kernel, tiles_per_seq=tiles_per_seq),
        grid=(n // tm,),
        in_specs=[row(d), row(c3.shape[1]),
                  pl.BlockSpec((BF16_SUBLANES, c3.shape[1]),
                               lambda i: (jnp.maximum(i * hb - 1, 0), 0)),
                  pl.BlockSpec((ym.shape[0], tm, ym.shape[2]), lambda i: (0, i, 0)),
                  row(yg.shape[1]),
                  _layer(wout, l), _layer(cw, l), _layer(cb, l), _layer(gx, l), _layer(wq, l),
                  mem_spec, mem_spec, _layer(wo, l)],
        out_specs=row(d),
        out_shape=jax.ShapeDtypeStruct((n, d), F32),
        compiler_params=_params("parallel"),
        name="mix_xattn",
    )(x, c3, c3, ym, yg, wout, cw, cb, gx, wq, memk, memv, wo)


def _ffn_kernel(x_ref, g_ref, wa_ref, wg_ref, cwa_ref, cwg_ref, cba_ref, cbg_ref, wd_ref,
                gf_ref, o_ref, hext_ref, acc_ref, ua_ref, ug_ref, *, tiles_per_seq, final):
    tm = x_ref.shape[0]
    halo = BF16_SUBLANES
    tn = ua_ref.shape[1]
    nc = wa_ref.shape[1] // tn
    first = (pl.program_id(0) % tiles_per_seq) == 0

    def cols(j):
        return pl.ds(pl.multiple_of(j * tn, tn), tn)

    @pl.when(first)
    def _zero_halo():
        hext_ref[0:halo, :] = jnp.zeros((halo, hext_ref.shape[1]), BF16)

    @pl.when(jnp.logical_not(first))
    def _carry_halo():
        hext_ref[0:halo, :] = hext_ref[tm:tm + halo, :]

    x = x_ref[...]
    hext_ref[halo:, :] = _rms(x, g_ref[...]).astype(BF16)
    acc_ref[...] = jnp.zeros_like(acc_ref)

    def up(j):
        hext = hext_ref[...]
        return _dot(hext, wa_ref[:, cols(j)]), _dot(hext, wg_ref[:, cols(j)])

    def conv(u, cw, cb):
        y = cw[0:1] * pltpu.roll(u, 2, 0) + cw[1:2] * pltpu.roll(u, 1, 0) + cw[2:3] * u + cb
        return y[F32_SUBLANES:]

    def down(j):
        src = slice(halo - F32_SUBLANES, halo + tm)
        a = conv(ua_ref[src, :], cwa_ref[:, cols(j)], cba_ref[:, cols(j)])
        g = conv(ug_ref[src, :], cwg_ref[:, cols(j)], cbg_ref[:, cols(j)])
        hid = ((a * (1.0 / (1.0 + jnp.exp(-a)))) * g).astype(BF16)
        acc_ref[...] += _dot(hid, wd_ref[cols(j), :])

    ua_ref[...], ug_ref[...] = up(0)

    def body(j, carry):
        ua, ug = up(j + 1)
        down(j)
        ua_ref[...] = ua
        ug_ref[...] = ug
        return carry

    lax.fori_loop(0, nc - 1, body, 0)
    down(nc - 1)
    y = x + acc_ref[...]
    if final:
        y = _rms(y, gf_ref[...])
    o_ref[...] = y


def _ffn(x, g, wa, wg, cwa, cwg, cba, cbg, wd, gf, l, *, tm, tn, seq, final):
    n, d = x.shape
    return pl.pallas_call(
        functools.partial(_ffn_kernel, tiles_per_seq=seq // tm, final=final),
        grid=(n // tm,),
        in_specs=[pl.BlockSpec((tm, d), lambda i: (i, 0)), _layer(g, l),
                  _layer(wa, l), _layer(wg, l), _layer(cwa, l), _layer(cwg, l),
                  _layer(cba, l), _layer(cbg, l), _layer(wd, l), _resident(gf.shape)],
        out_specs=pl.BlockSpec((tm, d), lambda i: (i, 0)),
        out_shape=jax.ShapeDtypeStruct((n, d), F32),
        scratch_shapes=[pltpu.VMEM((tm + BF16_SUBLANES, d), BF16),
                        pltpu.VMEM((tm, d), F32),
                        pltpu.VMEM((tm + BF16_SUBLANES, tn), F32),
                        pltpu.VMEM((tm + BF16_SUBLANES, tn), F32)],
        compiler_params=_params("arbitrary"),
        name="ffn",
    )(x, g, wa, wg, cwa, cwg, cba, cbg, wd, gf)


def _pad_cols(w, pad_to):
    return jnp.pad(w, [(0, 0)] * (w.ndim - 1) + [(0, pad_to - w.shape[-1])])


def kernel(x, mem, norm_mix_g, w_in, sc_conv_w, sc_conv_b, gla_w_gate, gla_b_gate, gla_norm_g,
           w_out, norm_xattn_g, norm_mem_g, xattn_wq, xattn_wkv, xattn_wo, norm_ffn_g,
           ffn_w_up, ffn_conv_w, ffn_conv_b, ffn_w_down, final_norm_g):
    bsz, seq, d = x.shape
    nl = w_in.shape[0]
    n = bsz * seq
    conv_ch = sc_conv_w.shape[-1]
    moba_w = N_MOBA_HEADS * HEAD_DIM
    gk_w = N_GLA_HEADS * GLA_DK
    gv_w = N_GLA_HEADS * GLA_DV
    d_ff = ffn_w_down.shape[1]
    ff_pad = -(-d_ff // FFN_CHUNK) * FFN_CHUNK
    tm = TOKEN_TILE

    widths = (3 * conv_ch, moba_w, moba_w, moba_w, 2 * gk_w, gv_w, gv_w, LANES)
    pair_major = (False, True, True, True, False, False, False, False)
    in_scales = (1.0, HEAD_DIM ** -0.5, 1.0, 1.0, 1.0, 1.0, 1.0)
    w_in_t = jnp.swapaxes(w_in, 1, 2)
    wg_p = jnp.pad(gla_w_gate.astype(BF16), ((0, 0), (LANES - GLA_GATE_RANK, 0), (0, 0)))
    gn_t = jnp.tile(gla_norm_g, (1, N_GLA_HEADS))
    w_out_b = w_out.astype(BF16)
    wq_b = (xattn_wq * (HEAD_DIM ** -0.5)).astype(BF16)
    wkv_b = xattn_wkv.astype(BF16)
    wo_b = xattn_wo.astype(BF16)
    wa = _pad_cols(ffn_w_up[..., :d_ff].astype(BF16), ff_pad)
    wgt = _pad_cols(ffn_w_up[..., d_ff:].astype(BF16), ff_pad)
    cwa = _pad_cols(ffn_conv_w[..., :d_ff], ff_pad)
    cwg = _pad_cols(ffn_conv_w[..., d_ff:], ff_pad)
    cba = _pad_cols(ffn_conv_b[:, None, :d_ff], ff_pad)
    cbg = _pad_cols(ffn_conv_b[:, None, d_ff:], ff_pad)
    wd = jnp.pad(ffn_w_down.astype(BF16), ((0, 0), (0, ff_pad - d_ff), (0, 0)))

    memk, memv = _mem_kv(mem, norm_mem_g[:, None, :], wkv_b)

    vec = lambda a: a[:, None, :]
    norm_mix_v, norm_xattn_v, norm_ffn_v = vec(norm_mix_g), vec(norm_xattn_g), vec(norm_ffn_g)
    gla_bg_v, gn_v, sc_conv_b_v = vec(gla_b_gate), vec(gn_t), vec(sc_conv_b)

    xf = x.reshape(n, d)
    for l in range(nl):
        c3, mq, mk, mv, gqk, gv, go, gl = _norm_inproj(
            xf, norm_mix_v, w_in_t, l, widths, pair_major, in_scales, tm)
        r3 = lambda a: a.reshape(bsz, seq, a.shape[-1])
        ym = _moba(mq, mk, mv, bsz)
        yg = _gla(r3(gqk), r3(gv), r3(go), r3(gl), wg_p, gla_bg_v, gn_v, l,
                  tt=GLA_TILE).reshape(n, gv_w)
        xf = _mix_xattn(xf, c3, ym, yg, w_out_b, sc_conv_w, sc_conv_b_v, norm_xattn_v, wq_b,
                        memk, memv, wo_b, l, tm=tm, seq=seq)
        xf = _ffn(xf, norm_ffn_v, wa, wgt, cwa, cwg, cba, cbg, wd, final_norm_g[None, :], l,
                  tm=tm, tn=FFN_CHUNK, seq=seq, final=(l == nl - 1))
    return xf.reshape(bsz, seq, d)
```

```python
import functools

import numpy as np
import jax
import jax.numpy as jnp
from jax import lax
from jax.experimental import pallas as pl
from jax.experimental.pallas import tpu as pltpu

F32 = jnp.float32
BF16 = jnp.bfloat16

EPS = 1e-6
HEAD_DIM = 64
N_MOBA_HEADS = 8
MOBA_BLOCK = 256
MOBA_TOPK = 3
N_GLA_HEADS = 4
GLA_DK = 32
GLA_DV = 64
GLA_CHUNK = 64
GLA_GATE_RANK = 16
GLA_GATE_NORMALIZER = 16.0
CONV_K = 3
N_XATTN_HEADS = 4
LOG2_HEAD_DIM = 6
LOG2_GLA_DK = 5
LOG2_MOBA_BLOCK = 8

LANES = 128
F32_SUBLANES = 8
BF16_SUBLANES = 16
VMEM_LIMIT = 56 * 1024 * 1024
TOKEN_TILE = 1024
FFN_CHUNK = 256
GLA_TILE = 1024
WT_CHUNK = 256

VT_ROWS = HEAD_DIM + 16
MASK_BIAS = -(2.0 ** 60)

_NT = (((1,), (1,)), ((), ()))
_TN = (((0,), (0,)), ((), ()))


def _dot(a, b):
    return jnp.dot(a, b, preferred_element_type=F32)


def _dot_nt(a, b):
    return lax.dot_general(a, b, _NT, preferred_element_type=F32)


def _dot_tn(a, b):
    return lax.dot_general(a, b, _TN, preferred_element_type=F32)


def _split_bf16(a):
    hi = a.astype(BF16)
    lo = (a - hi.astype(F32)).astype(BF16)
    return hi, lo


def _rms(x, g):
    r = lax.rsqrt(jnp.mean(x * x, axis=-1, keepdims=True) + EPS)
    return (x * r) * g


def _params(*sem):
    return pltpu.CompilerParams(dimension_semantics=sem, vmem_limit_bytes=VMEM_LIMIT)


def _resident(shape):
    nd = len(shape)
    return pl.BlockSpec(shape, lambda *_: (0,) * nd, pipeline_mode=pl.Buffered(1))


def _layer(arr, l):
    nd = arr.ndim
    return pl.BlockSpec((None,) + arr.shape[1:], lambda *_: (l,) + (0,) * (nd - 1),
                        pipeline_mode=pl.Buffered(1))


def _norm_inproj_kernel(x_ref, g_ref, wt_ref, *refs, scales, tail):
    out_refs, wb_ref = refs[:-1], refs[-1]
    in_w = wt_ref.shape[0]

    @pl.when(pl.program_id(0) == 0)
    def _prepare():
        off = 0
        for o_ref, scale in zip(out_refs[:-1], scales):
            wd = o_ref.shape[0] * o_ref.shape[2] if len(o_ref.shape) == 3 else o_ref.shape[-1]
            for c0 in range(off, off + wd, WT_CHUNK):
                blk = wt_ref[c0:c0 + WT_CHUNK, :]
                if scale != 1.0:
                    blk = blk * scale
                wb_ref[:, c0:c0 + WT_CHUNK] = blk.T.astype(BF16)
            off += wd
        wb_ref[:, off:off + tail] = wt_ref[in_w - tail:in_w, :].T.astype(BF16)

    h = _rms(x_ref[...], g_ref[...]).astype(BF16)
    off = 0
    for o_ref in out_refs:
        pair_major = len(o_ref.shape) == 3
        wd = o_ref.shape[0] * o_ref.shape[2] if pair_major else o_ref.shape[-1]
        res = _dot(h, wb_ref[:, off:off + wd]).astype(o_ref.dtype)
        if pair_major:
            for p in range(o_ref.shape[0]):
                o_ref[p] = res[:, p * o_ref.shape[2]:(p + 1) * o_ref.shape[2]]
        else:
            o_ref[...] = res
        off += wd


def _norm_inproj(x, g, wt, l, widths, pair_major, scales, tm):
    n, d = x.shape
    assert all(wd % WT_CHUNK == 0 for wd in widths[:-1]) and sum(widths[:-1]) <= wt.shape[1]
    specs, shapes = [], []
    for wd, pm in zip(widths, pair_major):
        if pm:
            specs.append(pl.BlockSpec((wd // LANES, tm, LANES), lambda i: (0, i, 0)))
            shapes.append(jax.ShapeDtypeStruct((wd // LANES, n, LANES), BF16))
        else:
            specs.append(pl.BlockSpec((tm, wd), lambda i: (i, 0)))
            shapes.append(jax.ShapeDtypeStruct((n, wd), BF16))
    return pl.pallas_call(
        functools.partial(_norm_inproj_kernel, scales=scales, tail=widths[-1]),
        grid=(n // tm,),
        in_specs=[pl.BlockSpec((tm, d), lambda i: (i, 0)), _layer(g, l), _layer(wt, l)],
        out_specs=specs,
        out_shape=shapes,
        scratch_shapes=[pltpu.VMEM((d, sum(widths)), BF16)],
        compiler_params=_params("arbitrary"),
        name="norm_inproj",
    )(x, g, wt)


def _moba_kernel(q_ref, k_ref, v_ref, kaux_ref, s256_ref, o_ref,
                 kaug_ref, vt_ref, kmean_ref, s_ref, p_ref, acc_ref, *, nblk, blk):
    j = pl.program_id(2)
    hd2 = 2 * HEAD_DIM
    tq = 2 * blk
    kt = 2 * blk

    @pl.when(j == 0)
    def _prepare():
        lane = lax.broadcasted_iota(jnp.int32, (blk, hd2), 1)
        aux = kaux_ref[0].astype(F32)
        ones = jnp.ones((VT_ROWS - HEAD_DIM, blk), BF16)
        for n in range(nblk):
            rows = slice(n * blk, (n + 1) * blk)
            kb = k_ref[rows, :]
            kaug_ref[rows, 0:hd2] = kb
            kaug_ref[rows, hd2:2 * hd2] = (aux + jnp.where(lane == n, 1.0, 0.0)).astype(BF16)
            vt = v_ref[rows, :].astype(F32).T.astype(BF16)
            for hh in range(2):
                vt_ref[hh, 0:HEAD_DIM, rows] = vt[hh * HEAD_DIM:(hh + 1) * HEAD_DIM]
                vt_ref[hh, HEAD_DIM:VT_ROWS, rows] = ones
            kmean_ref[n:n + 1, :] = jnp.sum(kb.astype(F32), axis=0, keepdims=True) * (1.0 / blk)

    qt = q_ref[...].astype(F32).T
    qt_b = qt.astype(BF16)
    km = kmean_ref[...]
    km_lane_head = lax.broadcasted_iota(jnp.int32, (nblk, hd2), 1) >> LOG2_HEAD_DIM
    qt_row_head = lax.broadcasted_iota(jnp.int32, (hd2, tq), 0) >> LOG2_HEAD_DIM
    nidx = lax.broadcasted_iota(jnp.int32, (nblk, tq), 0)
    qblk = 2 * j + (lax.broadcasted_iota(jnp.int32, (nblk, tq), 1) >> LOG2_MOBA_BLOCK)
    aux_row = lax.broadcasted_iota(jnp.int32, (hd2 - nblk, tq), 0)
    past = nidx < qblk

    qaug = []
    for hh in range(2):
        kmh = jnp.where(km_lane_head == hh, km, 0.0)
        kh, kl = _split_bf16(kmh)
        gate = _dot(kh, qt_b) + _dot(kl, qt_b)
        g = jnp.where(past, gate, -jnp.inf)
        rank = jnp.zeros((nblk, tq), F32)
        for m in range(nblk):
            gm = g[m:m + 1, :]
            rank = rank + jnp.where(m < nidx, jnp.where(gm >= g, 1.0, 0.0), jnp.where(gm > g, 1.0, 0.0))
        sel = (past & (rank < MOBA_TOPK)) | (nidx == qblk)
        s256 = s256_ref[0, hh:hh + 1, :]
        bias = jnp.where(sel, (nidx - qblk).astype(F32) * s256, MASK_BIAS)
        head_one = jnp.where(aux_row == hh, 1.0, 0.0)
        qaug.append(jnp.concatenate(
            [jnp.where(qt_row_head == hh, qt, 0.0), bias, head_one], axis=0).astype(BF16))
    qall = jnp.concatenate(qaug, axis=1)

    ncg = 4
    cg = [slice(c * blk, (c + 1) * blk) for c in range(ncg)]

    def softmax_c(st, m, c):
        mn = jnp.maximum(m, jnp.max(st, axis=0, keepdims=True))
        p_ref[:, cg[c]] = jnp.exp(st - mn).astype(BF16)
        return mn, jnp.exp(m - mn)

    def accumulate_c(n, alpha, c):
        off = pl.multiple_of(n * kt, kt)
        hh, a = c // 2, c % 2
        acc_ref[hh, :, a * blk:(a + 1) * blk] = (
            alpha * acc_ref[hh, :, a * blk:(a + 1) * blk]
            + _dot(vt_ref[hh, :, pl.ds(off, kt)], p_ref[:, cg[c]]))

    s_ref[...] = _dot(kaug_ref[0:kt, :], qall)
    p_ref[...] = jnp.zeros_like(p_ref)
    acc_ref[...] = jnp.zeros_like(acc_ref)
    m0 = tuple(jnp.full((1, blk), -jnp.inf, F32) for _ in range(ncg))
    a0 = tuple(jnp.ones((1, blk), F32) for _ in range(ncg))

    def body(i, carry):
        ms, alphas = carry
        off = pl.multiple_of((i + 1) * kt, kt)
        ktile = kaug_ref[pl.ds(off, kt), :]
        new_m, new_a, nxt = [], [], []
        for c in range(ncg):
            nxt.append(_dot(ktile, qall[:, cg[c]]))
            accumulate_c(jnp.maximum(i - 1, 0), alphas[c], c)
            mn, al = softmax_c(s_ref[:, cg[c]], ms[c], c)
            new_m.append(mn)
            new_a.append(al)
        for c in range(ncg):
            s_ref[:, cg[c]] = nxt[c]
        return tuple(new_m), tuple(new_a)

    ms, alphas = lax.fori_loop(0, j, body, (m0, a0))
    for c in range(ncg):
        accumulate_c(jnp.maximum(j - 1, 0), alphas[c], c)
    key_i = lax.broadcasted_iota(jnp.int32, (blk, blk), 0)
    qry_i = lax.broadcasted_iota(jnp.int32, (blk, blk), 1)
    causal = key_i <= qry_i
    off = pl.multiple_of(j * kt, kt)
    for c in range(ncg):
        hh, a = c // 2, c % 2
        parts = [jnp.where(causal, s_ref[0:blk, cg[c]], -jnp.inf)] if a == 0 else [
            s_ref[0:blk, cg[c]], jnp.where(causal, s_ref[blk:kt, cg[c]], -jnp.inf)]
        mn = ms[c]
        for st in parts:
            mn = jnp.maximum(mn, jnp.max(st, axis=0, keepdims=True))
        for r, st in enumerate(parts):
            p_ref[r * blk:(r + 1) * blk, cg[c]] = jnp.exp(st - mn).astype(BF16)
        nk = len(parts) * blk
        acc_ref[hh, :, a * blk:(a + 1) * blk] = (
            jnp.exp(ms[c] - mn) * acc_ref[hh, :, a * blk:(a + 1) * blk]
            + _dot(vt_ref[hh, :, pl.ds(off, nk)], p_ref[0:nk, cg[c]]))

    for a in range(2):
        ot = []
        for hh in range(2):
            acc = acc_ref[hh, :, a * blk:(a + 1) * blk]
            ot.append(acc[0:HEAD_DIM] * (1.0 / acc[HEAD_DIM:HEAD_DIM + 1]))
        o_ref[a * blk:(a + 1) * blk, :] = jnp.concatenate(ot, axis=0).T.astype(o_ref.dtype)


def _moba_tables(nblk):
    slopes = 2.0 ** (-8.0 * np.arange(1, N_MOBA_HEADS + 1) / N_MOBA_HEADS)
    npair = N_MOBA_HEADS // 2
    kaux = np.zeros((npair, MOBA_BLOCK, 2 * HEAD_DIM), np.float32)
    s256 = np.zeros((npair, 8, 2 * MOBA_BLOCK), np.float32)
    for p in range(npair):
        for hh in range(2):
            kaux[p, :, nblk + hh] = slopes[2 * p + hh] * np.arange(MOBA_BLOCK)
            s256[p, hh, :] = slopes[2 * p + hh] * MOBA_BLOCK
    return jnp.asarray(kaux, BF16), jnp.asarray(s256, F32)


def _moba(q, k, v, bsz):
    npair, n, hd2 = q.shape
    t = n // bsz
    blk = MOBA_BLOCK
    nblk = t // blk
    assert blk == 1 << LOG2_MOBA_BLOCK and t % (2 * blk) == 0 and nblk % 8 == 0 and nblk <= HEAD_DIM
    assert hd2 == 2 * HEAD_DIM
    tq = 2 * blk
    steps = t // tq
    kaux, s256 = _moba_tables(nblk)
    q_spec = pl.BlockSpec((None, tq, hd2), lambda bi, p, i: (p, bi * steps + i, 0))
    kv_spec = pl.BlockSpec((None, t, hd2), lambda bi, p, i: (p, bi, 0))
    return pl.pallas_call(
        functools.partial(_moba_kernel, nblk=nblk, blk=blk),
        grid=(bsz, npair, steps),
        in_specs=[q_spec, kv_spec, kv_spec,
                  pl.BlockSpec((1, blk, hd2), lambda bi, p, i: (p, 0, 0)),
                  pl.BlockSpec((1, 8, tq), lambda bi, p, i: (p, 0, 0))],
        out_specs=q_spec,
        out_shape=jax.ShapeDtypeStruct((npair, n, hd2), BF16),
        scratch_shapes=[pltpu.VMEM((t, 2 * hd2), BF16),
                        pltpu.VMEM((2, VT_ROWS, t), BF16),
                        pltpu.VMEM((nblk, hd2), F32),
                        pltpu.VMEM((2 * blk, 2 * tq), F32),
                        pltpu.VMEM((2 * blk, 2 * tq), BF16),
                        pltpu.VMEM((2, VT_ROWS, tq), F32)],
        compiler_params=_params("parallel", "parallel", "arbitrary"),
        name="moba",
    )(q, k, v, kaux, s256)


def _gla_kernel(qk_ref, v_ref, go_ref, gl_ref, wg_ref, bg_ref, gn_ref, o_ref, st_ref,
                *, sub, chunk):
    kw = N_GLA_HEADS * GLA_DK
    vw = N_GLA_HEADS * GLA_DV
    tt = qk_ref.shape[1]

    @pl.when(pl.program_id(1) == 0)
    def _reset():
        st_ref[...] = jnp.zeros_like(st_ref)

    r = lax.broadcasted_iota(jnp.int32, (sub, sub), 0)
    c = lax.broadcasted_iota(jnp.int32, (sub, sub), 1)
    same = (r >> LOG2_HEAD_DIM) == (c >> LOG2_HEAD_DIM)
    causal = same & (c <= r)
    ltri = jnp.where(causal, 1.0, 0.0).astype(BF16)
    st_diag = ((lax.broadcasted_iota(jnp.int32, (vw, kw), 0) >> LOG2_HEAD_DIM)
               == (lax.broadcasted_iota(jnp.int32, (vw, kw), 1) >> LOG2_GLA_DK))
    grp = jnp.where((lax.broadcasted_iota(jnp.int32, (vw, vw), 0) >> LOG2_HEAD_DIM)
                    == (lax.broadcasted_iota(jnp.int32, (vw, vw), 1) >> LOG2_HEAD_DIM),
                    1.0, 0.0).astype(BF16)
    khead = lax.broadcasted_iota(jnp.int32, (sub, kw), 1) >> LOG2_GLA_DK
    vhead = lax.broadcasted_iota(jnp.int32, (sub, vw), 1) >> LOG2_HEAD_DIM

    for s in range(tt // sub):
        rows = slice(s * sub, (s + 1) * sub)
        z = _dot(gl_ref[0, rows, :], wg_ref[...]) + bg_ref[...]
        la = (jnp.minimum(z, 0.0) - jnp.log(1.0 + jnp.exp(-jnp.abs(z)))) * (1.0 / GLA_GATE_NORMALIZER)
        la_hi, la_lo = _split_bf16(la)
        b = _dot(ltri, la_hi) + _dot(ltri, la_lo)
        tot = [b[(ci + 1) * chunk - 1:(ci + 1) * chunk, :] for ci in range(sub // chunk)]
        bl = jnp.concatenate([jnp.broadcast_to(t, (chunk, kw)) for t in tot], axis=0)
        q = qk_ref[0, rows, 0:kw].astype(F32)
        k = qk_ref[0, rows, kw:2 * kw].astype(F32)
        v = v_ref[0, rows, :]
        qd = (q * (jnp.exp(b) * (GLA_DK ** -0.5))).astype(BF16)
        kd = (k * jnp.exp(-b)).astype(BF16)
        kend = (k * jnp.exp(bl - b)).astype(BF16)

        o = jnp.zeros((sub, vw), F32)
        for h in range(N_GLA_HEADS):
            a = _dot_nt(jnp.where(khead == h, qd, jnp.zeros_like(qd)), kd)
            a = jnp.where(causal, a, 0.0).astype(BF16)
            o = o + _dot(a, jnp.where(vhead == h, v, jnp.zeros_like(v)))

        inter = []
        for ci in range(sub // chunk):
            cr = slice(ci * chunk, (ci + 1) * chunk)
            st = st_ref[...]
            inter.append(_dot_nt(qd[cr], st.astype(BF16)))
            ut = _dot_tn(v[cr], kend[cr])
            st_ref[...] = st * jnp.exp(tot[ci]) + jnp.where(st_diag, ut, 0.0)
        o = o + jnp.concatenate(inter, axis=0)

        o2_hi, o2_lo = _split_bf16(o * o)
        ms = (_dot(o2_hi, grp) + _dot(o2_lo, grp)) * (1.0 / GLA_DV)
        go = go_ref[0, rows, :].astype(F32)
        y = (o * lax.rsqrt(ms + EPS)) * gn_ref[...] * (go * (1.0 / (1.0 + jnp.exp(-go))))
        o_ref[0, rows, :] = y.astype(o_ref.dtype)


def _gla(qk, v, go, gl, wg, bg, gn, l, tt):
    b, t, _ = qk.shape
    kw = N_GLA_HEADS * GLA_DK
    vw = N_GLA_HEADS * GLA_DV
    spec = lambda w: pl.BlockSpec((1, tt, w), lambda bi, i: (bi, i, 0))
    return pl.pallas_call(
        functools.partial(_gla_kernel, sub=4 * GLA_CHUNK, chunk=GLA_CHUNK),
        grid=(b, t // tt),
        in_specs=[spec(2 * kw), spec(vw), spec(vw), spec(LANES),
                  _layer(wg, l), _layer(bg, l), _layer(gn, l)],
        out_specs=spec(vw),
        out_shape=jax.ShapeDtypeStruct((b, t, vw), BF16),
        scratch_shapes=[pltpu.VMEM((vw, kw), F32)],
        compiler_params=_params("parallel", "arbitrary"),
        name="gla",
    )(qk, v, go, gl, wg, bg, gn)


def _mem_kv_kernel(mem_ref, g_ref, w_ref, k_ref, v_ref):
    h = _rms(mem_ref[0], g_ref[0]).astype(BF16)
    kv = _dot(h, w_ref[0])
    wd = k_ref.shape[-1]
    k_ref[0, 0] = kv[:, :wd].astype(BF16)
    v_ref[0, 0] = kv[:, wd:].astype(BF16)


def _mem_kv(mem, g, wkv):
    b, m, d = mem.shape
    nl = wkv.shape[0]
    wd = wkv.shape[-1] // 2
    out = jax.ShapeDtypeStruct((nl, b, m, wd), BF16)
    ospec = pl.BlockSpec((1, 1, m, wd), lambda l, bi: (l, bi, 0, 0))
    return pl.pallas_call(
        _mem_kv_kernel,
        grid=(nl, b),
        in_specs=[pl.BlockSpec((1, m, d), lambda l, bi: (bi, 0, 0)),
                  pl.BlockSpec((1, 1, d), lambda l, bi: (l, 0, 0)),
                  pl.BlockSpec((1, d, 2 * wd), lambda l, bi: (l, 0, 0))],
        out_specs=[ospec, ospec],
        out_shape=[out, out],
        compiler_params=_params("parallel", "parallel"),
        name="mem_kv",
    )(mem, g, wkv)


def _mix_xattn_kernel(x_ref, c3_ref, c3h_ref, ym_ref, yg_ref, wout_ref, cw_ref, cb_ref,
                      gx_ref, wq_ref, mk_ref, mv_ref, wo_ref, o_ref, *, tiles_per_seq):
    tm = x_ref.shape[0]
    cc = cw_ref.shape[-1]
    nh = N_XATTN_HEADS
    first = (pl.program_id(0) % tiles_per_seq) == 0

    c3 = c3_ref[...].astype(F32)
    u = c3[:, cc:2 * cc] * c3[:, 2 * cc:3 * cc]
    halo = c3h_ref[...].astype(F32)
    uh = halo[:, cc:2 * cc] * halo[:, 2 * cc:3 * cc]
    uh = jnp.where(first, 0.0, uh)
    last1 = uh[BF16_SUBLANES - 1:BF16_SUBLANES, :]
    last2 = uh[BF16_SUBLANES - 2:BF16_SUBLANES - 1, :]
    row = lax.broadcasted_iota(jnp.int32, (tm, cc), 0)
    u1 = jnp.where(row == 0, last1, pltpu.roll(u, 1, 0))
    u2 = jnp.where(row == 0, last2, jnp.where(row == 1, last1, pltpu.roll(u, 2, 0)))
    cw = cw_ref[...]
    yc = c3[:, 0:cc] * (cw[0:1] * u2 + cw[1:2] * u1 + cw[2:3] * u + cb_ref[...])

    ym = jnp.concatenate([ym_ref[p] for p in range(ym_ref.shape[0])], axis=1)
    mw = ym.shape[-1]
    x1 = (x_ref[...]
          + _dot(yc.astype(BF16), wout_ref[0:cc, :])
          + _dot(ym, wout_ref[cc:cc + mw, :])
          + _dot(yg_ref[...], wout_ref[cc + mw:, :]))

    h = _rms(x1, gx_ref[...]).astype(BF16)
    q = _dot(h, wq_ref[...]).astype(BF16)
    k = mk_ref[...]
    v = mv_ref[...]
    qhead = lax.broadcasted_iota(jnp.int32, q.shape, 1) >> LOG2_HEAD_DIM
    vhead = lax.broadcasted_iota(jnp.int32, v.shape, 1) >> LOG2_HEAD_DIM
    o = jnp.zeros(q.shape, F32)
    for hd in range(nh):
        s = _dot_nt(jnp.where(qhead == hd, q, jnp.zeros_like(q)), k)
        p = jnp.exp(s - jnp.max(s, axis=-1, keepdims=True))
        l = jnp.sum(p, axis=-1, keepdims=True)
        pv = _dot(p.astype(BF16), jnp.where(vhead == hd, v, jnp.zeros_like(v)))
        o = o + pv * (1.0 / l)
    o_ref[...] = x1 + _dot(o.astype(BF16), wo_ref[...])


def _mix_xattn(x, c3, ym, yg, wout, cw, cb, gx, wq, memk, memv, wo, l, *, tm, seq):
    n, d = x.shape
    tiles_per_seq = seq // tm
    hb = tm // BF16_SUBLANES
    row = lambda w: pl.BlockSpec((tm, w), lambda i: (i, 0))
    mem_spec = pl.BlockSpec((None, None) + memk.shape[2:], lambda i: (l, i // tiles_per_seq, 0, 0))
    return pl.pallas_call(
        functools.partial(_mix_xattn_kernel, tiles_per_seq=tiles_per_seq),
        grid=(n // tm,),
        in_specs=[row(d), row(c3.shape[1]),
                  pl.BlockSpec((BF16_SUBLANES, c3.shape[1]),
                               lambda i: (jnp.maximum(i * hb - 1, 0), 0)),
                  pl.BlockSpec((ym.shape[0], tm, ym.shape[2]), lambda i: (0, i, 0)),
                  row(yg.shape[1]),
                  _layer(wout, l), _layer(cw, l), _layer(cb, l), _layer(gx, l), _layer(wq, l),
                  mem_spec, mem_spec, _layer(wo, l)],
        out_specs=row(d),
        out_shape=jax.ShapeDtypeStruct((n, d), F32),
        compiler_params=_params("parallel"),
        name="mix_xattn",
    )(x, c3, c3, ym, yg, wout, cw, cb, gx, wq, memk, memv, wo)


def _ffn_kernel(x_ref, g_ref, wa_ref, wg_ref, cwa_ref, cwg_ref, cba_ref, cbg_ref, wd_ref,
                gf_ref, o_ref, hext_ref, acc_ref, ua_ref, ug_ref, *, tiles_per_seq, final):
    tm = x_ref.shape[0]
    halo = BF16_SUBLANES
    tn = ua_ref.shape[1]
    nc = wa_ref.shape[1] // tn
    first = (pl.program_id(0) % tiles_per_seq) == 0

    def cols(j):
        return pl.ds(pl.multiple_of(j * tn, tn), tn)

    @pl.when(first)
    def _zero_halo():
        hext_ref[0:halo, :] = jnp.zeros((halo, hext_ref.shape[1]), BF16)

    @pl.when(jnp.logical_not(first))
    def _carry_halo():
        hext_ref[0:halo, :] = hext_ref[tm:tm + halo, :]

    x = x_ref[...]
    hext_ref[halo:, :] = _rms(x, g_ref[...]).astype(BF16)
    acc_ref[...] = jnp.zeros_like(acc_ref)

    def up(j):
        hext = hext_ref[...]
        return _dot(hext, wa_ref[:, cols(j)]), _dot(hext, wg_ref[:, cols(j)])

    def conv(u, cw, cb):
        y = cw[0:1] * pltpu.roll(u, 2, 0) + cw[1:2] * pltpu.roll(u, 1, 0) + cw[2:3] * u + cb
        return y[F32_SUBLANES:]

    def down(j):
        src = slice(halo - F32_SUBLANES, halo + tm)
        a = conv(ua_ref[src, :], cwa_ref[:, cols(j)], cba_ref[:, cols(j)])
        g = conv(ug_ref[src, :], cwg_ref[:, cols(j)], cbg_ref[:, cols(j)])
        h2 = 0.5 * a
        hid = ((h2 + h2 * jnp.tanh(h2)) * g).astype(BF16)
        acc_ref[...] += _dot(hid, wd_ref[cols(j), :])

    ua_ref[...], ug_ref[...] = up(0)

    def body(j, carry):
        ua, ug = up(j + 1)
        down(j)
        ua_ref[...] = ua
        ug_ref[...] = ug
        return carry

    lax.fori_loop(0, nc - 1, body, 0)
    down(nc - 1)
    y = x + acc_ref[...]
    if final:
        y = _rms(y, gf_ref[...])
    o_ref[...] = y


def _ffn(x, g, wa, wg, cwa, cwg, cba, cbg, wd, gf, l, *, tm, tn, seq, final):
    n, d = x.shape
    return pl.pallas_call(
        functools.partial(_ffn_kernel, tiles_per_seq=seq // tm, final=final),
        grid=(n // tm,),
        in_specs=[pl.BlockSpec((tm, d), lambda i: (i, 0)), _layer(g, l),
                  _layer(wa, l), _layer(wg, l), _layer(cwa, l), _layer(cwg, l),
                  _layer(cba, l), _layer(cbg, l), _layer(wd, l), _resident(gf.shape)],
        out_specs=pl.BlockSpec((tm, d), lambda i: (i, 0)),
        out_shape=jax.ShapeDtypeStruct((n, d), F32),
        scratch_shapes=[pltpu.VMEM((tm + BF16_SUBLANES, d), BF16),
                        pltpu.VMEM((tm, d), F32),
                        pltpu.VMEM((tm + BF16_SUBLANES, tn), F32),
                        pltpu.VMEM((tm + BF16_SUBLANES, tn), F32)],
        compiler_params=_params("arbitrary"),
        name="ffn",
    )(x, g, wa, wg, cwa, cwg, cba, cbg, wd, gf)


def _pad_cols(w, pad_to):
    return jnp.pad(w, [(0, 0)] * (w.ndim - 1) + [(0, pad_to - w.shape[-1])])


def _split_up_kernel(w_ref, a_ref, g_ref, *, d_ff):
    w = w_ref[...]
    zeros = jnp.zeros((w.shape[0], a_ref.shape[-1] - d_ff), BF16)
    a_ref[:, 0:d_ff] = w[:, 0:d_ff].astype(BF16)
    a_ref[:, d_ff:] = zeros
    g_ref[:, 0:d_ff] = w[:, d_ff:2 * d_ff].astype(BF16)
    g_ref[:, d_ff:] = zeros


def _split_up(w, d_ff, ff_pad, rows):
    nl, d, _ = w.shape
    out = jax.ShapeDtypeStruct((nl, d, ff_pad), BF16)
    ospec = pl.BlockSpec((None, rows, ff_pad), lambda l, i: (l, i, 0))
    return pl.pallas_call(
        functools.partial(_split_up_kernel, d_ff=d_ff),
        grid=(nl, d // rows),
        in_specs=[pl.BlockSpec((None, rows, 2 * d_ff), lambda l, i: (l, i, 0))],
        out_specs=[ospec, ospec],
        out_shape=[out, out],
        compiler_params=_params("parallel", "parallel"),
        name="ffn_up_prep",
    )(w)


def _pad_down_kernel(w_ref, o_ref):
    rows = w_ref.shape[0]
    o_ref[0:rows, :] = w_ref[...].astype(BF16)
    o_ref[rows:, :] = jnp.zeros((o_ref.shape[0] - rows, o_ref.shape[1]), BF16)


def _pad_down(w, ff_pad):
    nl, d_ff, d = w.shape
    return pl.pallas_call(
        _pad_down_kernel,
        grid=(nl,),
        in_specs=[pl.BlockSpec((None, d_ff, d), lambda l: (l, 0, 0))],
        out_specs=pl.BlockSpec((None, ff_pad, d), lambda l: (l, 0, 0)),
        out_shape=jax.ShapeDtypeStruct((nl, ff_pad, d), BF16),
        compiler_params=_params("parallel"),
        name="ffn_down_prep",
    )(w)


def kernel(x, mem, norm_mix_g, w_in, sc_conv_w, sc_conv_b, gla_w_gate, gla_b_gate, gla_norm_g,
           w_out, norm_xattn_g, norm_mem_g, xattn_wq, xattn_wkv, xattn_wo, norm_ffn_g,
           ffn_w_up, ffn_conv_w, ffn_conv_b, ffn_w_down, final_norm_g):
    bsz, seq, d = x.shape
    nl = w_in.shape[0]
    n = bsz * seq
    conv_ch = sc_conv_w.shape[-1]
    moba_w = N_MOBA_HEADS * HEAD_DIM
    gk_w = N_GLA_HEADS * GLA_DK
    gv_w = N_GLA_HEADS * GLA_DV
    d_ff = ffn_w_down.shape[1]
    ff_pad = -(-d_ff // FFN_CHUNK) * FFN_CHUNK
    tm = TOKEN_TILE

    widths = (3 * conv_ch, moba_w, moba_w, moba_w, 2 * gk_w, gv_w, gv_w, LANES)
    pair_major = (False, True, True, True, False, False, False, False)
    in_scales = (1.0, HEAD_DIM ** -0.5, 1.0, 1.0, 1.0, 1.0, 1.0)
    w_in_t = jnp.swapaxes(w_in, 1, 2)
    wg_p = jnp.pad(gla_w_gate.astype(BF16), ((0, 0), (LANES - GLA_GATE_RANK, 0), (0, 0)))
    gn_t = jnp.tile(gla_norm_g, (1, N_GLA_HEADS))
    w_out_b = w_out.astype(BF16)
    wq_b = (xattn_wq * (HEAD_DIM ** -0.5)).astype(BF16)
    wkv_b = xattn_wkv.astype(BF16)
    wo_b = xattn_wo.astype(BF16)
    wa, wgt = _split_up(ffn_w_up, d_ff, ff_pad, rows=FFN_CHUNK)
    cwa = _pad_cols(ffn_conv_w[..., :d_ff], ff_pad)
    cwg = _pad_cols(ffn_conv_w[..., d_ff:], ff_pad)
    cba = _pad_cols(ffn_conv_b[:, None, :d_ff], ff_pad)
    cbg = _pad_cols(ffn_conv_b[:, None, d_ff:], ff_pad)
    wd = _pad_down(ffn_w_down, ff_pad)

    memk, memv = _mem_kv(mem, norm_mem_g[:, None, :], wkv_b)

    vec = lambda a: a[:, None, :]
    norm_mix_v, norm_xattn_v, norm_ffn_v = vec(norm_mix_g), vec(norm_xattn_g), vec(norm_ffn_g)
    gla_bg_v, gn_v, sc_conv_b_v = vec(gla_b_gate), vec(gn_t), vec(sc_conv_b)

    xf = x.reshape(n, d)
    for l in range(nl):
        c3, mq, mk, mv, gqk, gv, go, gl = _norm_inproj(
            xf, norm_mix_v, w_in_t, l, widths, pair_major, in_scales, tm)
        r3 = lambda a: a.reshape(bsz, seq, a.shape[-1])
        ym = _moba(mq, mk, mv, bsz)
        yg = _gla(r3(gqk), r3(gv), r3(go), r3(gl), wg_p, gla_bg_v, gn_v, l,
                  tt=GLA_TILE).reshape(n, gv_w)
        xf = _mix_xattn(xf, c3, ym, yg, w_out_b, sc_conv_w, sc_conv_b_v, norm_xattn_v, wq_b,
                        memk, memv, wo_b, l, tm=tm, seq=seq)
        xf = _ffn(xf, norm_ffn_v, wa, wgt, cwa, cwg, cba, cbg, wd, final_norm_g[None, :], l,
                  tm=tm, tn=FFN_CHUNK, seq=seq, final=(l == nl - 1))
    return xf.reshape(bsz, seq, d)
```

```python
import functools

import numpy as np
import jax
import jax.numpy as jnp
from jax import lax
from jax.experimental import pallas as pl
from jax.experimental.pallas import tpu as pltpu

F32 = jnp.float32
BF16 = jnp.bfloat16

EPS = 1e-6
HEAD_DIM = 64
N_MOBA_HEADS = 8
MOBA_BLOCK = 256
MOBA_TOPK = 3
N_GLA_HEADS = 4
GLA_DK = 32
GLA_DV = 64
GLA_CHUNK = 64
GLA_GATE_RANK = 16
GLA_GATE_NORMALIZER = 16.0
CONV_K = 3
N_XATTN_HEADS = 4
LOG2_HEAD_DIM = 6
LOG2_GLA_DK = 5
LOG2_MOBA_BLOCK = 8

LANES = 128
F32_SUBLANES = 8
BF16_SUBLANES = 16
VMEM_LIMIT = 56 * 1024 * 1024
TOKEN_TILE = 1024
FFN_CHUNK = 256
GLA_TILE = 1024
WT_CHUNK = 256

VT_ROWS = HEAD_DIM + 16
MASK_BIAS = -(2.0 ** 60)

_NT = (((1,), (1,)), ((), ()))
_TN = (((0,), (0,)), ((), ()))


def _dot(a, b):
    return jnp.dot(a, b, preferred_element_type=F32)


def _dot_nt(a, b):
    return lax.dot_general(a, b, _NT, preferred_element_type=F32)


def _dot_tn(a, b):
    return lax.dot_general(a, b, _TN, preferred_element_type=F32)


def _split_bf16(a):
    hi = a.astype(BF16)
    lo = (a - hi.astype(F32)).astype(BF16)
    return hi, lo


def _rms(x, g):
    r = lax.rsqrt(jnp.mean(x * x, axis=-1, keepdims=True) + EPS)
    return (x * r) * g


def _params(*sem):
    return pltpu.CompilerParams(dimension_semantics=sem, vmem_limit_bytes=VMEM_LIMIT)


def _resident(shape):
    nd = len(shape)
    return pl.BlockSpec(shape, lambda *_: (0,) * nd, pipeline_mode=pl.Buffered(1))


def _layer(arr, l):
    nd = arr.ndim
    return pl.BlockSpec((None,) + arr.shape[1:], lambda *_: (l,) + (0,) * (nd - 1),
                        pipeline_mode=pl.Buffered(1))


def _norm_inproj_kernel(x_ref, g_ref, wt_ref, *refs, scales, tail):
    out_refs, wb_ref = refs[:-1], refs[-1]
    in_w = wt_ref.shape[0]

    @pl.when(pl.program_id(0) == 0)
    def _prepare():
        off = 0
        for o_ref, scale in zip(out_refs[:-1], scales):
            wd = o_ref.shape[0] * o_ref.shape[2] if len(o_ref.shape) == 3 else o_ref.shape[-1]
            for c0 in range(off, off + wd, WT_CHUNK):
                blk = wt_ref[c0:c0 + WT_CHUNK, :]
                if scale != 1.0:
                    blk = blk * scale
                wb_ref[:, c0:c0 + WT_CHUNK] = blk.T.astype(BF16)
            off += wd
        wb_ref[:, off:off + tail] = wt_ref[in_w - tail:in_w, :].T.astype(BF16)

    h = _rms(x_ref[...], g_ref[...]).astype(BF16)
    off = 0
    for o_ref in out_refs:
        pair_major = len(o_ref.shape) == 3
        wd = o_ref.shape[0] * o_ref.shape[2] if pair_major else o_ref.shape[-1]
        res = _dot(h, wb_ref[:, off:off + wd]).astype(o_ref.dtype)
        if pair_major:
            for p in range(o_ref.shape[0]):
                o_ref[p] = res[:, p * o_ref.shape[2]:(p + 1) * o_ref.shape[2]]
        else:
            o_ref[...] = res
        off += wd


def _norm_inproj(x, g, wt, l, widths, pair_major, scales, tm):
    n, d = x.shape
    assert all(wd % WT_CHUNK == 0 for wd in widths[:-1]) and sum(widths[:-1]) <= wt.shape[1]
    specs, shapes = [], []
    for wd, pm in zip(widths, pair_major):
        if pm:
            specs.append(pl.BlockSpec((wd // LANES, tm, LANES), lambda i: (0, i, 0)))
            shapes.append(jax.ShapeDtypeStruct((wd // LANES, n, LANES), BF16))
        else:
            specs.append(pl.BlockSpec((tm, wd), lambda i: (i, 0)))
            shapes.append(jax.ShapeDtypeStruct((n, wd), BF16))
    return pl.pallas_call(
        functools.partial(_norm_inproj_kernel, scales=scales, tail=widths[-1]),
        grid=(n // tm,),
        in_specs=[pl.BlockSpec((tm, d), lambda i: (i, 0)), _layer(g, l), _layer(wt, l)],
        out_specs=specs,
        out_shape=shapes,
        scratch_shapes=[pltpu.VMEM((d, sum(widths)), BF16)],
        compiler_params=_params("arbitrary"),
        name="norm_inproj",
    )(x, g, wt)


def _moba_kernel(q_ref, k_ref, v_ref, kaux_ref, s256_ref, o_ref,
                 kaug_ref, vt_ref, kmean_ref, s_ref, p_ref, acc_ref, *, nblk, blk):
    j = pl.program_id(2)
    hd2 = 2 * HEAD_DIM
    tq = 2 * blk
    kt = 2 * blk

    @pl.when(j == 0)
    def _prepare():
        lane = lax.broadcasted_iota(jnp.int32, (blk, hd2), 1)
        aux = kaux_ref[0].astype(F32)
        ones = jnp.ones((VT_ROWS - HEAD_DIM, blk), BF16)
        for n in range(nblk):
            rows = slice(n * blk, (n + 1) * blk)
            kb = k_ref[rows, :]
            kaug_ref[rows, 0:hd2] = kb
            kaug_ref[rows, hd2:2 * hd2] = (aux + jnp.where(lane == n, 1.0, 0.0)).astype(BF16)
            vt = v_ref[rows, :].astype(F32).T.astype(BF16)
            for hh in range(2):
                vt_ref[hh, 0:HEAD_DIM, rows] = vt[hh * HEAD_DIM:(hh + 1) * HEAD_DIM]
                vt_ref[hh, HEAD_DIM:VT_ROWS, rows] = ones
            kmean_ref[n:n + 1, :] = jnp.sum(kb.astype(F32), axis=0, keepdims=True) * (1.0 / blk)

    qt = q_ref[...].astype(F32).T
    qt_b = qt.astype(BF16)
    km = kmean_ref[...]
    km_lane_head = lax.broadcasted_iota(jnp.int32, (nblk, hd2), 1) >> LOG2_HEAD_DIM
    qt_row_head = lax.broadcasted_iota(jnp.int32, (hd2, tq), 0) >> LOG2_HEAD_DIM
    nidx = lax.broadcasted_iota(jnp.int32, (nblk, tq), 0)
    qblk = 2 * j + (lax.broadcasted_iota(jnp.int32, (nblk, tq), 1) >> LOG2_MOBA_BLOCK)
    aux_row = lax.broadcasted_iota(jnp.int32, (hd2 - nblk, tq), 0)
    past = nidx < qblk

    qaug = []
    for hh in range(2):
        kmh = jnp.where(km_lane_head == hh, km, 0.0)
        kh, kl = _split_bf16(kmh)
        gate = _dot(kh, qt_b) + _dot(kl, qt_b)
        g = jnp.where(past, gate, -jnp.inf)
        rank = jnp.zeros((nblk, tq), F32)
        for m in range(nblk):
            gm = g[m:m + 1, :]
            rank = rank + jnp.where(m < nidx, jnp.where(gm >= g, 1.0, 0.0), jnp.where(gm > g, 1.0, 0.0))
        sel = (past & (rank < MOBA_TOPK)) | (nidx == qblk)
        s256 = s256_ref[0, hh:hh + 1, :]
        bias = jnp.where(sel, (nidx - qblk).astype(F32) * s256, MASK_BIAS)
        head_one = jnp.where(aux_row == hh, 1.0, 0.0)
        qaug.append(jnp.concatenate(
            [jnp.where(qt_row_head == hh, qt, 0.0), bias, head_one], axis=0).astype(BF16))
    qall = jnp.concatenate(qaug, axis=1)

    ncg = 4
    cg = [slice(c * blk, (c + 1) * blk) for c in range(ncg)]

    def softmax_c(st, m, c):
        mn = jnp.maximum(m, jnp.max(st, axis=0, keepdims=True))
        p_ref[:, cg[c]] = jnp.exp(st - mn).astype(BF16)
        return mn, jnp.exp(m - mn)

    def accumulate_c(n, alpha, c):
        off = pl.multiple_of(n * kt, kt)
        hh, a = c // 2, c % 2
        acc_ref[hh, :, a * blk:(a + 1) * blk] = (
            alpha * acc_ref[hh, :, a * blk:(a + 1) * blk]
            + _dot(vt_ref[hh, :, pl.ds(off, kt)], p_ref[:, cg[c]]))

    s_ref[...] = _dot(kaug_ref[0:kt, :], qall)
    p_ref[...] = jnp.zeros_like(p_ref)
    acc_ref[...] = jnp.zeros_like(acc_ref)
    m0 = tuple(jnp.full((1, blk), -jnp.inf, F32) for _ in range(ncg))
    a0 = tuple(jnp.ones((1, blk), F32) for _ in range(ncg))

    def body(i, carry):
        ms, alphas = carry
        off = pl.multiple_of((i + 1) * kt, kt)
        ktile = kaug_ref[pl.ds(off, kt), :]
        new_m, new_a, nxt = [], [], []
        for c in range(ncg):
            nxt.append(_dot(ktile, qall[:, cg[c]]))
            accumulate_c(jnp.maximum(i - 1, 0), alphas[c], c)
            mn, al = softmax_c(s_ref[:, cg[c]], ms[c], c)
            new_m.append(mn)
            new_a.append(al)
        for c in range(ncg):
            s_ref[:, cg[c]] = nxt[c]
        return tuple(new_m), tuple(new_a)

    ms, alphas = lax.fori_loop(0, j, body, (m0, a0))
    for c in range(ncg):
        accumulate_c(jnp.maximum(j - 1, 0), alphas[c], c)
    key_i = lax.broadcasted_iota(jnp.int32, (blk, blk), 0)
    qry_i = lax.broadcasted_iota(jnp.int32, (blk, blk), 1)
    causal = key_i <= qry_i
    off = pl.multiple_of(j * kt, kt)
    for c in range(ncg):
        hh, a = c // 2, c % 2
        parts = [jnp.where(causal, s_ref[0:blk, cg[c]], -jnp.inf)] if a == 0 else [
            s_ref[0:blk, cg[c]], jnp.where(causal, s_ref[blk:kt, cg[c]], -jnp.inf)]
        mn = ms[c]
        for st in parts:
            mn = jnp.maximum(mn, jnp.max(st, axis=0, keepdims=True))
        for r, st in enumerate(parts):
            p_ref[r * blk:(r + 1) * blk, cg[c]] = jnp.exp(st - mn).astype(BF16)
        nk = len(parts) * blk
        acc_ref[hh, :, a * blk:(a + 1) * blk] = (
            jnp.exp(ms[c] - mn) * acc_ref[hh, :, a * blk:(a + 1) * blk]
            + _dot(vt_ref[hh, :, pl.ds(off, nk)], p_ref[0:nk, cg[c]]))

    for a in range(2):
        ot = []
        for hh in range(2):
            acc = acc_ref[hh, :, a * blk:(a + 1) * blk]
            ot.append(acc[0:HEAD_DIM] * (1.0 / acc[HEAD_DIM:HEAD_DIM + 1]))
        o_ref[a * blk:(a + 1) * blk, :] = jnp.concatenate(ot, axis=0).T.astype(o_ref.dtype)


def _moba_tables(nblk):
    slopes = 2.0 ** (-8.0 * np.arange(1, N_MOBA_HEADS + 1) / N_MOBA_HEADS)
    npair = N_MOBA_HEADS // 2
    kaux = np.zeros((npair, MOBA_BLOCK, 2 * HEAD_DIM), np.float32)
    s256 = np.zeros((npair, 8, 2 * MOBA_BLOCK), np.float32)
    for p in range(npair):
        for hh in range(2):
            kaux[p, :, nblk + hh] = slopes[2 * p + hh] * np.arange(MOBA_BLOCK)
            s256[p, hh, :] = slopes[2 * p + hh] * MOBA_BLOCK
    return jnp.asarray(kaux, BF16), jnp.asarray(s256, F32)


def _moba(q, k, v, bsz):
    npair, n, hd2 = q.shape
    t = n // bsz
    blk = MOBA_BLOCK
    nblk = t // blk
    assert blk == 1 << LOG2_MOBA_BLOCK and t % (2 * blk) == 0 and nblk % 8 == 0 and nblk <= HEAD_DIM
    assert hd2 == 2 * HEAD_DIM
    tq = 2 * blk
    steps = t // tq
    kaux, s256 = _moba_tables(nblk)
    q_spec = pl.BlockSpec((None, tq, hd2), lambda bi, p, i: (p, bi * steps + i, 0))
    kv_spec = pl.BlockSpec((None, t, hd2), lambda bi, p, i: (p, bi, 0))
    return pl.pallas_call(
        functools.partial(_moba_kernel, nblk=nblk, blk=blk),
        grid=(bsz, npair, steps),
        in_specs=[q_spec, kv_spec, kv_spec,
                  pl.BlockSpec((1, blk, hd2), lambda bi, p, i: (p, 0, 0)),
                  pl.BlockSpec((1, 8, tq), lambda bi, p, i: (p, 0, 0))],
        out_specs=q_spec,
        out_shape=jax.ShapeDtypeStruct((npair, n, hd2), BF16),
        scratch_shapes=[pltpu.VMEM((t, 2 * hd2), BF16),
                        pltpu.VMEM((2, VT_ROWS, t), BF16),
                        pltpu.VMEM((nblk, hd2), F32),
                        pltpu.VMEM((2 * blk, 2 * tq), F32),
                        pltpu.VMEM((2 * blk, 2 * tq), BF16),
                        pltpu.VMEM((2, VT_ROWS, tq), F32)],
        compiler_params=_params("parallel", "parallel", "arbitrary"),
        name="moba",
    )(q, k, v, kaux, s256)


def _gla_kernel(qk_ref, v_ref, go_ref, gl_ref, wg_ref, bg_ref, gn_ref, o_ref, st_ref,
                *, sub, chunk):
    kw = N_GLA_HEADS * GLA_DK
    vw = N_GLA_HEADS * GLA_DV
    tt = qk_ref.shape[1]

    @pl.when(pl.program_id(1) == 0)
    def _reset():
        st_ref[...] = jnp.zeros_like(st_ref)

    r = lax.broadcasted_iota(jnp.int32, (sub, sub), 0)
    c = lax.broadcasted_iota(jnp.int32, (sub, sub), 1)
    same = (r >> LOG2_HEAD_DIM) == (c >> LOG2_HEAD_DIM)
    causal = same & (c <= r)
    ltri = jnp.where(causal, 1.0, 0.0).astype(BF16)
    st_diag = ((lax.broadcasted_iota(jnp.int32, (vw, kw), 0) >> LOG2_HEAD_DIM)
               == (lax.broadcasted_iota(jnp.int32, (vw, kw), 1) >> LOG2_GLA_DK))
    grp = jnp.where((lax.broadcasted_iota(jnp.int32, (vw, vw), 0) >> LOG2_HEAD_DIM)
                    == (lax.broadcasted_iota(jnp.int32, (vw, vw), 1) >> LOG2_HEAD_DIM),
                    1.0, 0.0).astype(BF16)
    khead = lax.broadcasted_iota(jnp.int32, (sub, kw), 1) >> LOG2_GLA_DK
    vhead = lax.broadcasted_iota(jnp.int32, (sub, vw), 1) >> LOG2_HEAD_DIM

    for s in range(tt // sub):
        rows = slice(s * sub, (s + 1) * sub)
        z = _dot(gl_ref[0, rows, :], wg_ref[...]) + bg_ref[...]
        la = (jnp.minimum(z, 0.0) - jnp.log(1.0 + jnp.exp(-jnp.abs(z)))) * (1.0 / GLA_GATE_NORMALIZER)
        la_hi, la_lo = _split_bf16(la)
        b = _dot(ltri, la_hi) + _dot(ltri, la_lo)
        tot = [b[(ci + 1) * chunk - 1:(ci + 1) * chunk, :] for ci in range(sub // chunk)]
        bl = jnp.concatenate([jnp.broadcast_to(t, (chunk, kw)) for t in tot], axis=0)
        q = qk_ref[0, rows, 0:kw].astype(F32)
        k = qk_ref[0, rows, kw:2 * kw].astype(F32)
        v = v_ref[0, rows, :]
        qd = (q * (jnp.exp(b) * (GLA_DK ** -0.5))).astype(BF16)
        kd = (k * jnp.exp(-b)).astype(BF16)
        kend = (k * jnp.exp(bl - b)).astype(BF16)

        o = jnp.zeros((sub, vw), F32)
        for h in range(N_GLA_HEADS):
            a = _dot_nt(jnp.where(khead == h, qd, jnp.zeros_like(qd)), kd)
            a = jnp.where(causal, a, 0.0).astype(BF16)
            o = o + _dot(a, jnp.where(vhead == h, v, jnp.zeros_like(v)))

        inter = []
        for ci in range(sub // chunk):
            cr = slice(ci * chunk, (ci + 1) * chunk)
            st = st_ref[...]
            inter.append(_dot_nt(qd[cr], st.astype(BF16)))
            ut = _dot_tn(v[cr], kend[cr])
            st_ref[...] = st * jnp.exp(tot[ci]) + jnp.where(st_diag, ut, 0.0)
        o = o + jnp.concatenate(inter, axis=0)

        o2_hi, o2_lo = _split_bf16(o * o)
        ms = (_dot(o2_hi, grp) + _dot(o2_lo, grp)) * (1.0 / GLA_DV)
        go = go_ref[0, rows, :].astype(F32)
        y = (o * lax.rsqrt(ms + EPS)) * gn_ref[...] * (go * (1.0 / (1.0 + jnp.exp(-go))))
        o_ref[0, rows, :] = y.astype(o_ref.dtype)


def _gla(qk, v, go, gl, wg, bg, gn, l, tt):
    b, t, _ = qk.shape
    kw = N_GLA_HEADS * GLA_DK
    vw = N_GLA_HEADS * GLA_DV
    spec = lambda w: pl.BlockSpec((1, tt, w), lambda bi, i: (bi, i, 0))
    return pl.pallas_call(
        functools.partial(_gla_kernel, sub=4 * GLA_CHUNK, chunk=GLA_CHUNK),
        grid=(b, t // tt),
        in_specs=[spec(2 * kw), spec(vw), spec(vw), spec(LANES),
                  _layer(wg, l), _layer(bg, l), _layer(gn, l)],
        out_specs=spec(vw),
        out_shape=jax.ShapeDtypeStruct((b, t, vw), BF16),
        scratch_shapes=[pltpu.VMEM((vw, kw), F32)],
        compiler_params=_params("parallel", "arbitrary"),
        name="gla",
    )(qk, v, go, gl, wg, bg, gn)


def _mem_kv_kernel(mem_ref, g_ref, w_ref, k_ref, v_ref):
    h = _rms(mem_ref[0], g_ref[0]).astype(BF16)
    kv = _dot(h, w_ref[0])
    wd = k_ref.shape[-1]
    k_ref[0, 0] = kv[:, :wd].astype(BF16)
    v_ref[0, 0] = kv[:, wd:].astype(BF16)


def _mem_kv(mem, g, wkv):
    b, m, d = mem.shape
    nl = wkv.shape[0]
    wd = wkv.shape[-1] // 2
    out = jax.ShapeDtypeStruct((nl, b, m, wd), BF16)
    ospec = pl.BlockSpec((1, 1, m, wd), lambda l, bi: (l, bi, 0, 0))
    return pl.pallas_call(
        _mem_kv_kernel,
        grid=(nl, b),
        in_specs=[pl.BlockSpec((1, m, d), lambda l, bi: (bi, 0, 0)),
                  pl.BlockSpec((1, 1, d), lambda l, bi: (l, 0, 0)),
                  pl.BlockSpec((1, d, 2 * wd), lambda l, bi: (l, 0, 0))],
        out_specs=[ospec, ospec],
        out_shape=[out, out],
        compiler_params=_params("parallel", "parallel"),
        name="mem_kv",
    )(mem, g, wkv)


def _mix_xattn_kernel(x_ref, c3_ref, c3h_ref, ym_ref, yg_ref, wout_ref, cw_ref, cb_ref,
                      gx_ref, wq_ref, mk_ref, mv_ref, wo_ref, o_ref, *, tiles_per_seq):
    tm = x_ref.shape[0]
    cc = cw_ref.shape[-1]
    nh = N_XATTN_HEADS
    first = (pl.program_id(0) % tiles_per_seq) == 0

    c3 = c3_ref[...].astype(F32)
    u = c3[:, cc:2 * cc] * c3[:, 2 * cc:3 * cc]
    halo = c3h_ref[...].astype(F32)
    uh = halo[:, cc:2 * cc] * halo[:, 2 * cc:3 * cc]
    uh = jnp.where(first, 0.0, uh)
    last1 = uh[BF16_SUBLANES - 1:BF16_SUBLANES, :]
    last2 = uh[BF16_SUBLANES - 2:BF16_SUBLANES - 1, :]
    row = lax.broadcasted_iota(jnp.int32, (tm, cc), 0)
    u1 = jnp.where(row == 0, last1, pltpu.roll(u, 1, 0))
    u2 = jnp.where(row == 0, last2, jnp.where(row == 1, last1, pltpu.roll(u, 2, 0)))
    cw = cw_ref[...]
    yc = c3[:, 0:cc] * (cw[0:1] * u2 + cw[1:2] * u1 + cw[2:3] * u + cb_ref[...])

    ym = jnp.concatenate([ym_ref[p] for p in range(ym_ref.shape[0])], axis=1)
    mw = ym.shape[-1]
    x1 = (x_ref[...]
          + _dot(yc.astype(BF16), wout_ref[0:cc, :])
          + _dot(ym, wout_ref[cc:cc + mw, :])
          + _dot(yg_ref[...], wout_ref[cc + mw:, :]))

    h = _rms(x1, gx_ref[...]).astype(BF16)
    q = _dot(h, wq_ref[...]).astype(BF16)
    k = mk_ref[...]
    v = mv_ref[...]
    qhead = lax.broadcasted_iota(jnp.int32, q.shape, 1) >> LOG2_HEAD_DIM
    vhead = lax.broadcasted_iota(jnp.int32, v.shape, 1) >> LOG2_HEAD_DIM
    o = jnp.zeros(q.shape, F32)
    for hd in range(nh):
        s = _dot_nt(jnp.where(qhead == hd, q, jnp.zeros_like(q)), k)
        p = jnp.exp(s - jnp.max(s, axis=-1, keepdims=True))
        l = jnp.sum(p, axis=-1, keepdims=True)
        pv = _dot(p.astype(BF16), jnp.where(vhead == hd, v, jnp.zeros_like(v)))
        o = o + pv * (1.0 / l)
    o_ref[...] = x1 + _dot(o.astype(BF16), wo_ref[...])


def _mix_xattn(x, c3, ym, yg, wout, cw, cb, gx, wq, memk, memv, wo, l, *, tm, seq):
    n, d = x.shape
    tiles_per_seq = seq // tm
    hb = tm // BF16_SUBLANES
    row = lambda w: pl.BlockSpec((tm, w), lambda i: (i, 0))
    mem_spec = pl.BlockSpec((None, None) + memk.shape[2:], lambda i: (l, i // tiles_per_seq, 0, 0))
    return pl.pallas_call(
        functools.partial(_mix_xattn_kernel, tiles_per_seq=tiles_per_seq),
        grid=(n // tm,),
        in_specs=[row(d), row(c3.shape[1]),
                  pl.BlockSpec((BF16_SUBLANES, c3.shape[1]),
                               lambda i: (jnp.maximum(i * hb - 1, 0), 0)),
                  pl.BlockSpec((ym.shape[0], tm, ym.shape[2]), lambda i: (0, i, 0)),
                  row(yg.shape[1]),
                  _layer(wout, l), _layer(cw, l), _layer(cb, l), _layer(gx, l), _layer(wq, l),
                  mem_spec, mem_spec, _layer(wo, l)],
        out_specs=row(d),
        out_shape=jax.ShapeDtypeStruct((n, d), F32),
        compiler_params=_params("parallel"),
        name="mix_xattn",
    )(x, c3, c3, ym, yg, wout, cw, cb, gx, wq, memk, memv, wo)


def _ffn_kernel(x_ref, g_ref, wa_ref, wg_ref, cwa_ref, cwg_ref, cba_ref, cbg_ref, wd_ref,
                gf_ref, o_ref, hext_ref, hid_ref, ua_ref, ug_ref, *, tiles_per_seq, final):
    tm = x_ref.shape[0]
    halo = BF16_SUBLANES
    tn = ua_ref.shape[1]
    nc = wa_ref.shape[1] // tn
    first = (pl.program_id(0) % tiles_per_seq) == 0

    def cols(j):
        return pl.ds(pl.multiple_of(j * tn, tn), tn)

    @pl.when(first)
    def _zero_halo():
        hext_ref[0:halo, :] = jnp.zeros((halo, hext_ref.shape[1]), BF16)

    @pl.when(jnp.logical_not(first))
    def _carry_halo():
        hext_ref[0:halo, :] = hext_ref[tm:tm + halo, :]

    x = x_ref[...]
    hext_ref[halo:, :] = _rms(x, g_ref[...]).astype(BF16)

    def up(j):
        hext = hext_ref[...]
        return _dot(hext, wa_ref[:, cols(j)]), _dot(hext, wg_ref[:, cols(j)])

    def conv(u, cw, cb):
        y = cw[0:1] * pltpu.roll(u, 2, 0) + cw[1:2] * pltpu.roll(u, 1, 0) + cw[2:3] * u + cb
        return y[F32_SUBLANES:]

    def gate(j):
        src = slice(halo - F32_SUBLANES, halo + tm)
        a = conv(ua_ref[src, :], cwa_ref[:, cols(j)], cba_ref[:, cols(j)])
        g = conv(ug_ref[src, :], cwg_ref[:, cols(j)], cbg_ref[:, cols(j)])
        h2 = 0.5 * a
        hid_ref[:, cols(j)] = ((h2 + h2 * jnp.tanh(h2)) * g).astype(BF16)

    ua_ref[...], ug_ref[...] = up(0)

    def body(j, carry):
        ua, ug = up(j + 1)
        gate(j)
        ua_ref[...] = ua
        ug_ref[...] = ug
        return carry

    lax.fori_loop(0, nc - 1, body, 0)
    gate(nc - 1)
    y = x + _dot(hid_ref[...], wd_ref[...])
    if final:
        y = _rms(y, gf_ref[...])
    o_ref[...] = y


def _ffn(x, g, wa, wg, cwa, cwg, cba, cbg, wd, gf, l, *, tm, tn, seq, final):
    n, d = x.shape
    return pl.pallas_call(
        functools.partial(_ffn_kernel, tiles_per_seq=seq // tm, final=final),
        grid=(n // tm,),
        in_specs=[pl.BlockSpec((tm, d), lambda i: (i, 0)), _layer(g, l),
                  _layer(wa, l), _layer(wg, l), _layer(cwa, l), _layer(cwg, l),
                  _layer(cba, l), _layer(cbg, l), _layer(wd, l), _resident(gf.shape)],
        out_specs=pl.BlockSpec((tm, d), lambda i: (i, 0)),
        out_shape=jax.ShapeDtypeStruct((n, d), F32),
        scratch_shapes=[pltpu.VMEM((tm + BF16_SUBLANES, d), BF16),
                        pltpu.VMEM((tm, wa.shape[-1]), BF16),
                        pltpu.VMEM((tm + BF16_SUBLANES, tn), F32),
                        pltpu.VMEM((tm + BF16_SUBLANES, tn), F32)],
        compiler_params=_params("arbitrary"),
        name="ffn",
    )(x, g, wa, wg, cwa, cwg, cba, cbg, wd, gf)


def _pad_cols(w, pad_to):
    return jnp.pad(w, [(0, 0)] * (w.ndim - 1) + [(0, pad_to - w.shape[-1])])


def _split_up_kernel(w_ref, a_ref, g_ref, *, d_ff):
    w = w_ref[...]
    zeros = jnp.zeros((w.shape[0], a_ref.shape[-1] - d_ff), BF16)
    a_ref[:, 0:d_ff] = w[:, 0:d_ff].astype(BF16)
    a_ref[:, d_ff:] = zeros
    g_ref[:, 0:d_ff] = w[:, d_ff:2 * d_ff].astype(BF16)
    g_ref[:, d_ff:] = zeros


def _split_up(w, d_ff, ff_pad, rows):
    nl, d, _ = w.shape
    out = jax.ShapeDtypeStruct((nl, d, ff_pad), BF16)
    ospec = pl.BlockSpec((None, rows, ff_pad), lambda l, i: (l, i, 0))
    return pl.pallas_call(
        functools.partial(_split_up_kernel, d_ff=d_ff),
        grid=(nl, d // rows),
        in_specs=[pl.BlockSpec((None, rows, 2 * d_ff), lambda l, i: (l, i, 0))],
        out_specs=[ospec, ospec],
        out_shape=[out, out],
        compiler_params=_params("parallel", "parallel"),
        name="ffn_up_prep",
    )(w)


def _pad_down_kernel(w_ref, o_ref):
    rows = w_ref.shape[0]
    o_ref[0:rows, :] = w_ref[...].astype(BF16)
    o_ref[rows:, :] = jnp.zeros((o_ref.shape[0] - rows, o_ref.shape[1]), BF16)


def _pad_down(w, ff_pad):
    nl, d_ff, d = w.shape
    return pl.pallas_call(
        _pad_down_kernel,
        grid=(nl,),
        in_specs=[pl.BlockSpec((None, d_ff, d), lambda l: (l, 0, 0))],
        out_specs=pl.BlockSpec((None, ff_pad, d), lambda l: (l, 0, 0)),
        out_shape=jax.ShapeDtypeStruct((nl, ff_pad, d), BF16),
        compiler_params=_params("parallel"),
        name="ffn_down_prep",
    )(w)


def kernel(x, mem, norm_mix_g, w_in, sc_conv_w, sc_conv_b, gla_w_gate, gla_b_gate, gla_norm_g,
           w_out, norm_xattn_g, norm_mem_g, xattn_wq, xattn_wkv, xattn_wo, norm_ffn_g,
           ffn_w_up, ffn_conv_w, ffn_conv_b, ffn_w_down, final_norm_g):
    bsz, seq, d = x.shape
    nl = w_in.shape[0]
    n = bsz * seq
    conv_ch = sc_conv_w.shape[-1]
    moba_w = N_MOBA_HEADS * HEAD_DIM
    gk_w = N_GLA_HEADS * GLA_DK
    gv_w = N_GLA_HEADS * GLA_DV
    d_ff = ffn_w_down.shape[1]
    ff_pad = -(-d_ff // FFN_CHUNK) * FFN_CHUNK
    tm = TOKEN_TILE

    widths = (3 * conv_ch, moba_w, moba_w, moba_w, 2 * gk_w, gv_w, gv_w, LANES)
    pair_major = (False, True, True, True, False, False, False, False)
    in_scales = (1.0, HEAD_DIM ** -0.5, 1.0, 1.0, 1.0, 1.0, 1.0)
    w_in_t = jnp.swapaxes(w_in, 1, 2)
    wg_p = jnp.pad(gla_w_gate.astype(BF16), ((0, 0), (LANES - GLA_GATE_RANK, 0), (0, 0)))
    gn_t = jnp.tile(gla_norm_g, (1, N_GLA_HEADS))
    w_out_b = w_out.astype(BF16)
    wq_b = (xattn_wq * (HEAD_DIM ** -0.5)).astype(BF16)
    wkv_b = xattn_wkv.astype(BF16)
    wo_b = xattn_wo.astype(BF16)
    wa, wgt = _split_up(ffn_w_up, d_ff, ff_pad, rows=FFN_CHUNK)
    cwa = _pad_cols(ffn_conv_w[..., :d_ff], ff_pad)
    cwg = _pad_cols(ffn_conv_w[..., d_ff:], ff_pad)
    cba = _pad_cols(ffn_conv_b[:, None, :d_ff], ff_pad)
    cbg = _pad_cols(ffn_conv_b[:, None, d_ff:], ff_pad)
    wd = _pad_down(ffn_w_down, ff_pad)

    memk, memv = _mem_kv(mem, norm_mem_g[:, None, :], wkv_b)

    vec = lambda a: a[:, None, :]
    norm_mix_v, norm_xattn_v, norm_ffn_v = vec(norm_mix_g), vec(norm_xattn_g), vec(norm_ffn_g)
    gla_bg_v, gn_v, sc_conv_b_v = vec(gla_b_gate), vec(gn_t), vec(sc_conv_b)

    xf = x.reshape(n, d)
    for l in range(nl):
        c3, mq, mk, mv, gqk, gv, go, gl = _norm_inproj(
            xf, norm_mix_v, w_in_t, l, widths, pair_major, in_scales, tm)
        r3 = lambda a: a.reshape(bsz, seq, a.shape[-1])
        ym = _moba(mq, mk, mv, bsz)
        yg = _gla(r3(gqk), r3(gv), r3(go), r3(gl), wg_p, gla_bg_v, gn_v, l,
                  tt=GLA_TILE).reshape(n, gv_w)
        xf = _mix_xattn(xf, c3, ym, yg, w_out_b, sc_conv_w, sc_conv_b_v, norm_xattn_v, wq_b,
                        memk, memv, wo_b, l, tm=tm, seq=seq)
        xf = _ffn(xf, norm_ffn_v, wa, wgt, cwa, cwg, cba, cbg, wd, final_norm_g[None, :], l,
                  tm=tm, tn=FFN_CHUNK, seq=seq, final=(l == nl - 1))
    return xf.reshape(bsz, seq, d)
```

```python
import functools

import numpy as np
import jax
import jax.numpy as jnp
from jax import lax
from jax.experimental import pallas as pl
from jax.experimental.pallas import tpu as pltpu

F32 = jnp.float32
BF16 = jnp.bfloat16

EPS = 1e-6
HEAD_DIM = 64
N_MOBA_HEADS = 8
MOBA_BLOCK = 256
MOBA_TOPK = 3
N_GLA_HEADS = 4
GLA_DK = 32
GLA_DV = 64
GLA_CHUNK = 64
GLA_GATE_RANK = 16
GLA_GATE_NORMALIZER = 16.0
CONV_K = 3
N_XATTN_HEADS = 4
LOG2_HEAD_DIM = 6
LOG2_GLA_DK = 5
LOG2_MOBA_BLOCK = 8

LANES = 128
F32_SUBLANES = 8
BF16_SUBLANES = 16
VMEM_LIMIT = 56 * 1024 * 1024
TOKEN_TILE = 1024
FFN_CHUNK = 256
GLA_TILE = 1024
WT_CHUNK = 256

VT_ROWS = HEAD_DIM + 16
MASK_BIAS = -(2.0 ** 60)

_NT = (((1,), (1,)), ((), ()))
_TN = (((0,), (0,)), ((), ()))


def _dot(a, b):
    return jnp.dot(a, b, preferred_element_type=F32)


def _dot_nt(a, b):
    return lax.dot_general(a, b, _NT, preferred_element_type=F32)


def _dot_tn(a, b):
    return lax.dot_general(a, b, _TN, preferred_element_type=F32)


def _split_bf16(a):
    hi = a.astype(BF16)
    lo = (a - hi.astype(F32)).astype(BF16)
    return hi, lo


def _rms(x, g):
    r = lax.rsqrt(jnp.mean(x * x, axis=-1, keepdims=True) + EPS)
    return (x * r) * g


def _params(*sem):
    return pltpu.CompilerParams(dimension_semantics=sem, vmem_limit_bytes=VMEM_LIMIT)


def _resident(shape):
    nd = len(shape)
    return pl.BlockSpec(shape, lambda *_: (0,) * nd, pipeline_mode=pl.Buffered(1))


def _layer(arr, l):
    nd = arr.ndim
    return pl.BlockSpec((None,) + arr.shape[1:], lambda *_: (l,) + (0,) * (nd - 1),
                        pipeline_mode=pl.Buffered(1))


def _norm_inproj_kernel(x_ref, g_ref, wt_ref, *refs, scales, tail):
    out_refs, wb_ref = refs[:-1], refs[-1]
    in_w = wt_ref.shape[0]

    @pl.when(pl.program_id(0) == 0)
    def _prepare():
        off = 0
        for o_ref, scale in zip(out_refs[:-1], scales):
            wd = o_ref.shape[0] * o_ref.shape[2] if len(o_ref.shape) == 3 else o_ref.shape[-1]
            for c0 in range(off, off + wd, WT_CHUNK):
                blk = wt_ref[c0:c0 + WT_CHUNK, :]
                if scale != 1.0:
                    blk = blk * scale
                wb_ref[:, c0:c0 + WT_CHUNK] = blk.T.astype(BF16)
            off += wd
        wb_ref[:, off:off + tail] = wt_ref[in_w - tail:in_w, :].T.astype(BF16)

    h = _rms(x_ref[...], g_ref[...]).astype(BF16)
    off = 0
    for o_ref in out_refs:
        pair_major = len(o_ref.shape) == 3
        wd = o_ref.shape[0] * o_ref.shape[2] if pair_major else o_ref.shape[-1]
        res = _dot(h, wb_ref[:, off:off + wd]).astype(o_ref.dtype)
        if pair_major:
            for p in range(o_ref.shape[0]):
                o_ref[p] = res[:, p * o_ref.shape[2]:(p + 1) * o_ref.shape[2]]
        else:
            o_ref[...] = res
        off += wd


def _norm_inproj(x, g, wt, l, widths, pair_major, scales, tm):
    n, d = x.shape
    assert all(wd % WT_CHUNK == 0 for wd in widths[:-1]) and sum(widths[:-1]) <= wt.shape[1]
    specs, shapes = [], []
    for wd, pm in zip(widths, pair_major):
        if pm:
            specs.append(pl.BlockSpec((wd // LANES, tm, LANES), lambda i: (0, i, 0)))
            shapes.append(jax.ShapeDtypeStruct((wd // LANES, n, LANES), BF16))
        else:
            specs.append(pl.BlockSpec((tm, wd), lambda i: (i, 0)))
            shapes.append(jax.ShapeDtypeStruct((n, wd), BF16))
    return pl.pallas_call(
        functools.partial(_norm_inproj_kernel, scales=scales, tail=widths[-1]),
        grid=(n // tm,),
        in_specs=[pl.BlockSpec((tm, d), lambda i: (i, 0)), _layer(g, l), _layer(wt, l)],
        out_specs=specs,
        out_shape=shapes,
        scratch_shapes=[pltpu.VMEM((d, sum(widths)), BF16)],
        compiler_params=_params("arbitrary"),
        name="norm_inproj",
    )(x, g, wt)


def _moba_kernel(q_ref, k_ref, v_ref, kaux_ref, s256_ref, o_ref,
                 kaug_ref, vt_ref, kmean_ref, s_ref, p_ref, acc_ref, *, nblk, blk):
    j = pl.program_id(2)
    hd2 = 2 * HEAD_DIM
    tq = 2 * blk
    kt = 2 * blk

    @pl.when(j == 0)
    def _prepare():
        lane = lax.broadcasted_iota(jnp.int32, (blk, hd2), 1)
        aux = kaux_ref[0].astype(F32)
        ones = jnp.ones((VT_ROWS - HEAD_DIM, blk), BF16)
        for n in range(nblk):
            rows = slice(n * blk, (n + 1) * blk)
            kb = k_ref[rows, :]
            kaug_ref[rows, 0:hd2] = kb
            kaug_ref[rows, hd2:2 * hd2] = (aux + jnp.where(lane == n, 1.0, 0.0)).astype(BF16)
            vt = v_ref[rows, :].astype(F32).T.astype(BF16)
            for hh in range(2):
                vt_ref[hh, 0:HEAD_DIM, rows] = vt[hh * HEAD_DIM:(hh + 1) * HEAD_DIM]
                vt_ref[hh, HEAD_DIM:VT_ROWS, rows] = ones
            kmean_ref[n:n + 1, :] = jnp.sum(kb.astype(F32), axis=0, keepdims=True) * (1.0 / blk)

    qt = q_ref[...].astype(F32).T
    qt_b = qt.astype(BF16)
    km = kmean_ref[...]
    km_lane_head = lax.broadcasted_iota(jnp.int32, (nblk, hd2), 1) >> LOG2_HEAD_DIM
    qt_row_head = lax.broadcasted_iota(jnp.int32, (hd2, tq), 0) >> LOG2_HEAD_DIM
    nidx = lax.broadcasted_iota(jnp.int32, (nblk, tq), 0)
    qblk = 2 * j + (lax.broadcasted_iota(jnp.int32, (nblk, tq), 1) >> LOG2_MOBA_BLOCK)
    aux_row = lax.broadcasted_iota(jnp.int32, (hd2 - nblk, tq), 0)
    past = nidx < qblk

    qaug = []
    for hh in range(2):
        kmh = jnp.where(km_lane_head == hh, km, 0.0)
        kh, kl = _split_bf16(kmh)
        gate = _dot(kh, qt_b) + _dot(kl, qt_b)
        g = jnp.where(past, gate, -jnp.inf)
        rank = jnp.zeros((nblk, tq), F32)
        for m in range(nblk):
            gm = g[m:m + 1, :]
            rank = rank + jnp.where(m < nidx, jnp.where(gm >= g, 1.0, 0.0), jnp.where(gm > g, 1.0, 0.0))
        sel = (past & (rank < MOBA_TOPK)) | (nidx == qblk)
        s256 = s256_ref[0, hh:hh + 1, :]
        bias = jnp.where(sel, (nidx - qblk).astype(F32) * s256, MASK_BIAS)
        head_one = jnp.where(aux_row == hh, 1.0, 0.0)
        qaug.append(jnp.concatenate(
            [jnp.where(qt_row_head == hh, qt, 0.0), bias, head_one], axis=0).astype(BF16))
    qall = jnp.concatenate(qaug, axis=1)

    ncg = 4
    cg = [slice(c * blk, (c + 1) * blk) for c in range(ncg)]

    def softmax_c(st, m, c):
        mn = jnp.maximum(m, jnp.max(st, axis=0, keepdims=True))
        p_ref[:, cg[c]] = jnp.exp(st - mn).astype(BF16)
        return mn, jnp.exp(m - mn)

    def accumulate_c(n, alpha, c):
        off = pl.multiple_of(n * kt, kt)
        hh, a = c // 2, c % 2
        acc_ref[hh, :, a * blk:(a + 1) * blk] = (
            alpha * acc_ref[hh, :, a * blk:(a + 1) * blk]
            + _dot(vt_ref[hh, :, pl.ds(off, kt)], p_ref[:, cg[c]]))

    s_ref[...] = _dot(kaug_ref[0:kt, :], qall)
    p_ref[...] = jnp.zeros_like(p_ref)
    acc_ref[...] = jnp.zeros_like(acc_ref)
    m0 = tuple(jnp.full((1, blk), -jnp.inf, F32) for _ in range(ncg))
    a0 = tuple(jnp.ones((1, blk), F32) for _ in range(ncg))

    def body(i, carry):
        ms, alphas = carry
        off = pl.multiple_of((i + 1) * kt, kt)
        ktile = kaug_ref[pl.ds(off, kt), :]
        new_m, new_a, nxt = [], [], []
        for c in range(ncg):
            nxt.append(_dot(ktile, qall[:, cg[c]]))
            accumulate_c(jnp.maximum(i - 1, 0), alphas[c], c)
            mn, al = softmax_c(s_ref[:, cg[c]], ms[c], c)
            new_m.append(mn)
            new_a.append(al)
        for c in range(ncg):
            s_ref[:, cg[c]] = nxt[c]
        return tuple(new_m), tuple(new_a)

    ms, alphas = lax.fori_loop(0, j, body, (m0, a0))
    for c in range(ncg):
        accumulate_c(jnp.maximum(j - 1, 0), alphas[c], c)
    key_i = lax.broadcasted_iota(jnp.int32, (blk, blk), 0)
    qry_i = lax.broadcasted_iota(jnp.int32, (blk, blk), 1)
    causal = key_i <= qry_i
    off = pl.multiple_of(j * kt, kt)
    for c in range(ncg):
        hh, a = c // 2, c % 2
        parts = [jnp.where(causal, s_ref[0:blk, cg[c]], -jnp.inf)] if a == 0 else [
            s_ref[0:blk, cg[c]], jnp.where(causal, s_ref[blk:kt, cg[c]], -jnp.inf)]
        mn = ms[c]
        for st in parts:
            mn = jnp.maximum(mn, jnp.max(st, axis=0, keepdims=True))
        for r, st in enumerate(parts):
            p_ref[r * blk:(r + 1) * blk, cg[c]] = jnp.exp(st - mn).astype(BF16)
        nk = len(parts) * blk
        acc_ref[hh, :, a * blk:(a + 1) * blk] = (
            jnp.exp(ms[c] - mn) * acc_ref[hh, :, a * blk:(a + 1) * blk]
            + _dot(vt_ref[hh, :, pl.ds(off, nk)], p_ref[0:nk, cg[c]]))

    for a in range(2):
        ot = []
        for hh in range(2):
            acc = acc_ref[hh, :, a * blk:(a + 1) * blk]
            ot.append(acc[0:HEAD_DIM] * (1.0 / acc[HEAD_DIM:HEAD_DIM + 1]))
        o_ref[a * blk:(a + 1) * blk, :] = jnp.concatenate(ot, axis=0).T.astype(o_ref.dtype)


def _moba_tables(nblk):
    slopes = 2.0 ** (-8.0 * np.arange(1, N_MOBA_HEADS + 1) / N_MOBA_HEADS)
    npair = N_MOBA_HEADS // 2
    kaux = np.zeros((npair, MOBA_BLOCK, 2 * HEAD_DIM), np.float32)
    s256 = np.zeros((npair, 8, 2 * MOBA_BLOCK), np.float32)
    for p in range(npair):
        for hh in range(2):
            kaux[p, :, nblk + hh] = slopes[2 * p + hh] * np.arange(MOBA_BLOCK)
            s256[p, hh, :] = slopes[2 * p + hh] * MOBA_BLOCK
    return jnp.asarray(kaux, BF16), jnp.asarray(s256, F32)


def _moba(q, k, v, bsz):
    npair, n, hd2 = q.shape
    t = n // bsz
    blk = MOBA_BLOCK
    nblk = t // blk
    assert blk == 1 << LOG2_MOBA_BLOCK and t % (2 * blk) == 0 and nblk % 8 == 0 and nblk <= HEAD_DIM
    assert hd2 == 2 * HEAD_DIM
    tq = 2 * blk
    steps = t // tq
    kaux, s256 = _moba_tables(nblk)
    q_spec = pl.BlockSpec((None, tq, hd2), lambda bi, p, i: (p, bi * steps + i, 0))
    kv_spec = pl.BlockSpec((None, t, hd2), lambda bi, p, i: (p, bi, 0))
    return pl.pallas_call(
        functools.partial(_moba_kernel, nblk=nblk, blk=blk),
        grid=(bsz, npair, steps),
        in_specs=[q_spec, kv_spec, kv_spec,
                  pl.BlockSpec((1, blk, hd2), lambda bi, p, i: (p, 0, 0)),
                  pl.BlockSpec((1, 8, tq), lambda bi, p, i: (p, 0, 0))],
        out_specs=q_spec,
        out_shape=jax.ShapeDtypeStruct((npair, n, hd2), BF16),
        scratch_shapes=[pltpu.VMEM((t, 2 * hd2), BF16),
                        pltpu.VMEM((2, VT_ROWS, t), BF16),
                        pltpu.VMEM((nblk, hd2), F32),
                        pltpu.VMEM((2 * blk, 2 * tq), F32),
                        pltpu.VMEM((2 * blk, 2 * tq), BF16),
                        pltpu.VMEM((2, VT_ROWS, tq), F32)],
        compiler_params=_params("parallel", "parallel", "arbitrary"),
        name="moba",
    )(q, k, v, kaux, s256)


def _gla_kernel(qk_ref, v_ref, go_ref, gl_ref, wg_ref, bg_ref, gn_ref, o_ref, st_ref,
                *, sub, chunk):
    kw = N_GLA_HEADS * GLA_DK
    vw = N_GLA_HEADS * GLA_DV
    tt = qk_ref.shape[1]

    @pl.when(pl.program_id(1) == 0)
    def _reset():
        st_ref[...] = jnp.zeros_like(st_ref)

    r = lax.broadcasted_iota(jnp.int32, (sub, sub), 0)
    c = lax.broadcasted_iota(jnp.int32, (sub, sub), 1)
    same = (r >> LOG2_HEAD_DIM) == (c >> LOG2_HEAD_DIM)
    causal = same & (c <= r)
    ltri = jnp.where(causal, 1.0, 0.0).astype(BF16)
    st_diag = ((lax.broadcasted_iota(jnp.int32, (vw, kw), 0) >> LOG2_HEAD_DIM)
               == (lax.broadcasted_iota(jnp.int32, (vw, kw), 1) >> LOG2_GLA_DK))
    grp = jnp.where((lax.broadcasted_iota(jnp.int32, (vw, vw), 0) >> LOG2_HEAD_DIM)
                    == (lax.broadcasted_iota(jnp.int32, (vw, vw), 1) >> LOG2_HEAD_DIM),
                    1.0, 0.0).astype(BF16)
    khead = lax.broadcasted_iota(jnp.int32, (sub, kw), 1) >> LOG2_GLA_DK
    vhead = lax.broadcasted_iota(jnp.int32, (sub, vw), 1) >> LOG2_HEAD_DIM

    for s in range(tt // sub):
        rows = slice(s * sub, (s + 1) * sub)
        z = _dot(gl_ref[0, rows, :], wg_ref[...]) + bg_ref[...]
        la = (jnp.minimum(z, 0.0) - jnp.log(1.0 + jnp.exp(-jnp.abs(z)))) * (1.0 / GLA_GATE_NORMALIZER)
        la_hi, la_lo = _split_bf16(la)
        b = _dot(ltri, la_hi) + _dot(ltri, la_lo)
        tot = [b[(ci + 1) * chunk - 1:(ci + 1) * chunk, :] for ci in range(sub // chunk)]
        bl = jnp.concatenate([jnp.broadcast_to(t, (chunk, kw)) for t in tot], axis=0)
        q = qk_ref[0, rows, 0:kw].astype(F32)
        k = qk_ref[0, rows, kw:2 * kw].astype(F32)
        v = v_ref[0, rows, :]
        qd = (q * (jnp.exp(b) * (GLA_DK ** -0.5))).astype(BF16)
        kd = (k * jnp.exp(-b)).astype(BF16)
        kend = (k * jnp.exp(bl - b)).astype(BF16)

        o = jnp.zeros((sub, vw), F32)
        for h in range(N_GLA_HEADS):
            a = _dot_nt(jnp.where(khead == h, qd, jnp.zeros_like(qd)), kd)
            a = jnp.where(causal, a, 0.0).astype(BF16)
            o = o + _dot(a, jnp.where(vhead == h, v, jnp.zeros_like(v)))

        inter = []
        for ci in range(sub // chunk):
            cr = slice(ci * chunk, (ci + 1) * chunk)
            st = st_ref[...]
            inter.append(_dot_nt(qd[cr], st.astype(BF16)))
            ut = _dot_tn(v[cr], kend[cr])
            st_ref[...] = st * jnp.exp(tot[ci]) + jnp.where(st_diag, ut, 0.0)
        o = o + jnp.concatenate(inter, axis=0)

        o2_hi, o2_lo = _split_bf16(o * o)
        ms = (_dot(o2_hi, grp) + _dot(o2_lo, grp)) * (1.0 / GLA_DV)
        go = go_ref[0, rows, :].astype(F32)
        y = (o * lax.rsqrt(ms + EPS)) * gn_ref[...] * (go * (1.0 / (1.0 + jnp.exp(-go))))
        o_ref[0, rows, :] = y.astype(o_ref.dtype)


def _gla(qk, v, go, gl, wg, bg, gn, l, tt):
    b, t, _ = qk.shape
    kw = N_GLA_HEADS * GLA_DK
    vw = N_GLA_HEADS * GLA_DV
    spec = lambda w: pl.BlockSpec((1, tt, w), lambda bi, i: (bi, i, 0))
    return pl.pallas_call(
        functools.partial(_gla_kernel, sub=4 * GLA_CHUNK, chunk=GLA_CHUNK),
        grid=(b, t // tt),
        in_specs=[spec(2 * kw), spec(vw), spec(vw), spec(LANES),
                  _layer(wg, l), _layer(bg, l), _layer(gn, l)],
        out_specs=spec(vw),
        out_shape=jax.ShapeDtypeStruct((b, t, vw), BF16),
        scratch_shapes=[pltpu.VMEM((vw, kw), F32)],
        compiler_params=_params("parallel", "arbitrary"),
        name="gla",
    )(qk, v, go, gl, wg, bg, gn)


def _mem_kv_kernel(mem_ref, g_ref, w_ref, k_ref, v_ref):
    h = _rms(mem_ref[0], g_ref[0]).astype(BF16)
    kv = _dot(h, w_ref[0])
    wd = k_ref.shape[-1]
    k_ref[0, 0] = kv[:, :wd].astype(BF16)
    v_ref[0, 0] = kv[:, wd:].astype(BF16)


def _mem_kv(mem, g, wkv):
    b, m, d = mem.shape
    nl = wkv.shape[0]
    wd = wkv.shape[-1] // 2
    out = jax.ShapeDtypeStruct((nl, b, m, wd), BF16)
    ospec = pl.BlockSpec((1, 1, m, wd), lambda l, bi: (l, bi, 0, 0))
    return pl.pallas_call(
        _mem_kv_kernel,
        grid=(nl, b),
        in_specs=[pl.BlockSpec((1, m, d), lambda l, bi: (bi, 0, 0)),
                  pl.BlockSpec((1, 1, d), lambda l, bi: (l, 0, 0)),
                  pl.BlockSpec((1, d, 2 * wd), lambda l, bi: (l, 0, 0))],
        out_specs=[ospec, ospec],
        out_shape=[out, out],
        compiler_params=_params("parallel", "parallel"),
        name="mem_kv",
    )(mem, g, wkv)


def _mix_xattn_kernel(x_ref, c3_ref, c3h_ref, ym_ref, yg_ref, wout_ref, cw_ref, cb_ref,
                      gx_ref, wq_ref, mk_ref, mv_ref, wo_ref, o_ref, *, tiles_per_seq):
    tm = x_ref.shape[0]
    cc = cw_ref.shape[-1]
    nh = N_XATTN_HEADS
    first = (pl.program_id(0) % tiles_per_seq) == 0

    c3 = c3_ref[...].astype(F32)
    u = c3[:, cc:2 * cc] * c3[:, 2 * cc:3 * cc]
    halo = c3h_ref[...].astype(F32)
    uh = halo[:, cc:2 * cc] * halo[:, 2 * cc:3 * cc]
    uh = jnp.where(first, 0.0, uh)
    last1 = uh[BF16_SUBLANES - 1:BF16_SUBLANES, :]
    last2 = uh[BF16_SUBLANES - 2:BF16_SUBLANES - 1, :]
    row = lax.broadcasted_iota(jnp.int32, (tm, cc), 0)
    u1 = jnp.where(row == 0, last1, pltpu.roll(u, 1, 0))
    u2 = jnp.where(row == 0, last2, jnp.where(row == 1, last1, pltpu.roll(u, 2, 0)))
    cw = cw_ref[...]
    yc = c3[:, 0:cc] * (cw[0:1] * u2 + cw[1:2] * u1 + cw[2:3] * u + cb_ref[...])

    ym = jnp.concatenate([ym_ref[p] for p in range(ym_ref.shape[0])], axis=1)
    mw = ym.shape[-1]
    x1 = (x_ref[...]
          + _dot(yc.astype(BF16), wout_ref[0:cc, :])
          + _dot(ym, wout_ref[cc:cc + mw, :])
          + _dot(yg_ref[...], wout_ref[cc + mw:, :]))

    h = _rms(x1, gx_ref[...]).astype(BF16)
    q = _dot(h, wq_ref[...]).astype(BF16)
    k = mk_ref[...]
    v = mv_ref[...]
    qhead = lax.broadcasted_iota(jnp.int32, q.shape, 1) >> LOG2_HEAD_DIM
    vhead = lax.broadcasted_iota(jnp.int32, v.shape, 1) >> LOG2_HEAD_DIM
    o = jnp.zeros(q.shape, F32)
    for hd in range(nh):
        s = _dot_nt(jnp.where(qhead == hd, q, jnp.zeros_like(q)), k)
        p = jnp.exp(s - jnp.max(s, axis=-1, keepdims=True))
        l = jnp.sum(p, axis=-1, keepdims=True)
        pv = _dot(p.astype(BF16), jnp.where(vhead == hd, v, jnp.zeros_like(v)))
        o = o + pv * (1.0 / l)
    o_ref[...] = x1 + _dot(o.astype(BF16), wo_ref[...])


def _mix_xattn(x, c3, ym, yg, wout, cw, cb, gx, wq, memk, memv, wo, l, *, tm, seq):
    n, d = x.shape
    tiles_per_seq = seq // tm
    hb = tm // BF16_SUBLANES
    row = lambda w: pl.BlockSpec((tm, w), lambda i: (i, 0))
    mem_spec = pl.BlockSpec((None, None) + memk.shape[2:], lambda i: (l, i // tiles_per_seq, 0, 0))
    return pl.pallas_call(
        functools.partial(_mix_xattn_kernel, tiles_per_seq=tiles_per_seq),
        grid=(n // tm,),
        in_specs=[row(d), row(c3.shape[1]),
                  pl.BlockSpec((BF16_SUBLANES, c3.shape[1]),
                               lambda i: (jnp.maximum(i * hb - 1, 0), 0)),
                  pl.BlockSpec((ym.shape[0], tm, ym.shape[2]), lambda i: (0, i, 0)),
                  row(yg.shape[1]),
                  _layer(wout, l), _layer(cw, l), _layer(cb, l), _layer(gx, l), _layer(wq, l),
                  mem_spec, mem_spec, _layer(wo, l)],
        out_specs=row(d),
        out_shape=jax.ShapeDtypeStruct((n, d), F32),
        compiler_params=_params("parallel"),
        name="mix_xattn",
    )(x, c3, c3, ym, yg, wout, cw, cb, gx, wq, memk, memv, wo)


def _ffn_kernel(x_ref, g_ref, wa_ref, wg_ref, cwa_ref, cwg_ref, cba_ref, cbg_ref, wd_ref,
                gf_ref, o_ref, hext_ref, hid_ref, ua_ref, ug_ref, *, tiles_per_seq, final):
    tm = x_ref.shape[0]
    halo = BF16_SUBLANES
    tn = ua_ref.shape[1]
    nc = wa_ref.shape[1] // tn
    first = (pl.program_id(0) % tiles_per_seq) == 0

    def cols(j):
        return pl.ds(pl.multiple_of(j * tn, tn), tn)

    @pl.when(first)
    def _zero_halo():
        hext_ref[0:halo, :] = jnp.zeros((halo, hext_ref.shape[1]), BF16)

    @pl.when(jnp.logical_not(first))
    def _carry_halo():
        hext_ref[0:halo, :] = hext_ref[tm:tm + halo, :]

    x = x_ref[...]
    hext_ref[halo:, :] = _rms(x, g_ref[...]).astype(BF16)

    def up(j):
        hext = hext_ref[...]
        return _dot(hext, wa_ref[:, cols(j)]), _dot(hext, wg_ref[:, cols(j)])

    def conv(u, cw, cb):
        y = cw[0:1] * pltpu.roll(u, 2, 0) + cw[1:2] * pltpu.roll(u, 1, 0) + cw[2:3] * u + cb
        return y[F32_SUBLANES:]

    def gate(j):
        src = slice(halo - F32_SUBLANES, halo + tm)
        a = conv(ua_ref[src, :], cwa_ref[:, cols(j)], cba_ref[:, cols(j)])
        g = conv(ug_ref[src, :], cwg_ref[:, cols(j)], cbg_ref[:, cols(j)])
        h2 = (0.5 * a).astype(BF16)
        hid_ref[:, cols(j)] = (h2 + h2 * jnp.tanh(h2)) * g.astype(BF16)

    ua_ref[...], ug_ref[...] = up(0)

    def body(j, carry):
        ua, ug = up(j + 1)
        gate(j)
        ua_ref[...] = ua
        ug_ref[...] = ug
        return carry

    lax.fori_loop(0, nc - 1, body, 0)
    gate(nc - 1)
    y = x + _dot(hid_ref[...], wd_ref[...])
    if final:
        y = _rms(y, gf_ref[...])
    o_ref[...] = y


def _ffn(x, g, wa, wg, cwa, cwg, cba, cbg, wd, gf, l, *, tm, tn, seq, final):
    n, d = x.shape
    return pl.pallas_call(
        functools.partial(_ffn_kernel, tiles_per_seq=seq // tm, final=final),
        grid=(n // tm,),
        in_specs=[pl.BlockSpec((tm, d), lambda i: (i, 0)), _layer(g, l),
                  _layer(wa, l), _layer(wg, l), _layer(cwa, l), _layer(cwg, l),
                  _layer(cba, l), _layer(cbg, l), _layer(wd, l), _resident(gf.shape)],
        out_specs=pl.BlockSpec((tm, d), lambda i: (i, 0)),
        out_shape=jax.ShapeDtypeStruct((n, d), F32),
        scratch_shapes=[pltpu.VMEM((tm + BF16_SUBLANES, d), BF16),
                        pltpu.VMEM((tm, wa.shape[-1]), BF16),
                        pltpu.VMEM((tm + BF16_SUBLANES, tn), F32),
                        pltpu.VMEM((tm + BF16_SUBLANES, tn), F32)],
        compiler_params=_params("arbitrary"),
        name="ffn",
    )(x, g, wa, wg, cwa, cwg, cba, cbg, wd, gf)


def _pad_cols(w, pad_to):
    return jnp.pad(w, [(0, 0)] * (w.ndim - 1) + [(0, pad_to - w.shape[-1])])


def _split_up_kernel(w_ref, a_ref, g_ref, *, d_ff):
    w = w_ref[...]
    zeros = jnp.zeros((w.shape[0], a_ref.shape[-1] - d_ff), BF16)
    a_ref[:, 0:d_ff] = w[:, 0:d_ff].astype(BF16)
    a_ref[:, d_ff:] = zeros
    g_ref[:, 0:d_ff] = w[:, d_ff:2 * d_ff].astype(BF16)
    g_ref[:, d_ff:] = zeros


def _split_up(w, d_ff, ff_pad, rows):
    nl, d, _ = w.shape
    out = jax.ShapeDtypeStruct((nl, d, ff_pad), BF16)
    ospec = pl.BlockSpec((None, rows, ff_pad), lambda l, i: (l, i, 0))
    return pl.pallas_call(
        functools.partial(_split_up_kernel, d_ff=d_ff),
        grid=(nl, d // rows),
        in_specs=[pl.BlockSpec((None, rows, 2 * d_ff), lambda l, i: (l, i, 0))],
        out_specs=[ospec, ospec],
        out_shape=[out, out],
        compiler_params=_params("parallel", "parallel"),
        name="ffn_up_prep",
    )(w)


def _pad_down_kernel(w_ref, o_ref):
    rows = w_ref.shape[0]
    o_ref[0:rows, :] = w_ref[...].astype(BF16)
    o_ref[rows:, :] = jnp.zeros((o_ref.shape[0] - rows, o_ref.shape[1]), BF16)


def _pad_down(w, ff_pad):
    nl, d_ff, d = w.shape
    return pl.pallas_call(
        _pad_down_kernel,
        grid=(nl,),
        in_specs=[pl.BlockSpec((None, d_ff, d), lambda l: (l, 0, 0))],
        out_specs=pl.BlockSpec((None, ff_pad, d), lambda l: (l, 0, 0)),
        out_shape=jax.ShapeDtypeStruct((nl, ff_pad, d), BF16),
        compiler_params=_params("parallel"),
        name="ffn_down_prep",
    )(w)


def kernel(x, mem, norm_mix_g, w_in, sc_conv_w, sc_conv_b, gla_w_gate, gla_b_gate, gla_norm_g,
           w_out, norm_xattn_g, norm_mem_g, xattn_wq, xattn_wkv, xattn_wo, norm_ffn_g,
           ffn_w_up, ffn_conv_w, ffn_conv_b, ffn_w_down, final_norm_g):
    bsz, seq, d = x.shape
    nl = w_in.shape[0]
    n = bsz * seq
    conv_ch = sc_conv_w.shape[-1]
    moba_w = N_MOBA_HEADS * HEAD_DIM
    gk_w = N_GLA_HEADS * GLA_DK
    gv_w = N_GLA_HEADS * GLA_DV
    d_ff = ffn_w_down.shape[1]
    ff_pad = -(-d_ff // FFN_CHUNK) * FFN_CHUNK
    tm = TOKEN_TILE

    widths = (3 * conv_ch, moba_w, moba_w, moba_w, 2 * gk_w, gv_w, gv_w, LANES)
    pair_major = (False, True, True, True, False, False, False, False)
    in_scales = (1.0, HEAD_DIM ** -0.5, 1.0, 1.0, 1.0, 1.0, 1.0)
    w_in_t = jnp.swapaxes(w_in, 1, 2)
    wg_p = jnp.pad(gla_w_gate.astype(BF16), ((0, 0), (LANES - GLA_GATE_RANK, 0), (0, 0)))
    gn_t = jnp.tile(gla_norm_g, (1, N_GLA_HEADS))
    w_out_b = w_out.astype(BF16)
    wq_b = (xattn_wq * (HEAD_DIM ** -0.5)).astype(BF16)
    wkv_b = xattn_wkv.astype(BF16)
    wo_b = xattn_wo.astype(BF16)
    wa, wgt = _split_up(ffn_w_up, d_ff, ff_pad, rows=FFN_CHUNK)
    cwa = _pad_cols(ffn_conv_w[..., :d_ff], ff_pad)
    cwg = _pad_cols(ffn_conv_w[..., d_ff:], ff_pad)
    cba = _pad_cols(ffn_conv_b[:, None, :d_ff], ff_pad)
    cbg = _pad_cols(ffn_conv_b[:, None, d_ff:], ff_pad)
    wd = _pad_down(ffn_w_down, ff_pad)

    memk, memv = _mem_kv(mem, norm_mem_g[:, None, :], wkv_b)

    vec = lambda a: a[:, None, :]
    norm_mix_v, norm_xattn_v, norm_ffn_v = vec(norm_mix_g), vec(norm_xattn_g), vec(norm_ffn_g)
    gla_bg_v, gn_v, sc_conv_b_v = vec(gla_b_gate), vec(gn_t), vec(sc_conv_b)

    xf = x.reshape(n, d)
    for l in range(nl):
        c3, mq, mk, mv, gqk, gv, go, gl = _norm_inproj(
            xf, norm_mix_v, w_in_t, l, widths, pair_major, in_scales, tm)
        r3 = lambda a: a.reshape(bsz, seq, a.shape[-1])
        ym = _moba(mq, mk, mv, bsz)
        yg = _gla(r3(gqk), r3(gv), r3(go), r3(gl), wg_p, gla_bg_v, gn_v, l,
                  tt=GLA_TILE).reshape(n, gv_w)
        xf = _mix_xattn(xf, c3, ym, yg, w_out_b, sc_conv_w, sc_conv_b_v, norm_xattn_v, wq_b,
                        memk, memv, wo_b, l, tm=tm, seq=seq)
        xf = _ffn(xf, norm_ffn_v, wa, wgt, cwa, cwg, cba, cbg, wd, final_norm_g[None, :], l,
                  tm=tm, tn=FFN_CHUNK, seq=seq, final=(l == nl - 1))
    return xf.reshape(bsz, seq, d)
```

```python
import functools

import numpy as np
import jax
import jax.numpy as jnp
from jax import lax
from jax.experimental import pallas as pl
from jax.experimental.pallas import tpu as pltpu

F32 = jnp.float32
BF16 = jnp.bfloat16

EPS = 1e-6
HEAD_DIM = 64
N_MOBA_HEADS = 8
MOBA_BLOCK = 256
MOBA_TOPK = 3
N_GLA_HEADS = 4
GLA_DK = 32
GLA_DV = 64
GLA_CHUNK = 64
GLA_GATE_RANK = 16
GLA_GATE_NORMALIZER = 16.0
CONV_K = 3
N_XATTN_HEADS = 4
LOG2_HEAD_DIM = 6
LOG2_GLA_DK = 5
LOG2_MOBA_BLOCK = 8

LANES = 128
F32_SUBLANES = 8
BF16_SUBLANES = 16
VMEM_LIMIT = 56 * 1024 * 1024
TOKEN_TILE = 1024
FFN_CHUNK = 256
GLA_TILE = 1024
WT_CHUNK = 256

VT_ROWS = HEAD_DIM + 16
MASK_BIAS = -(2.0 ** 60)

_NT = (((1,), (1,)), ((), ()))
_TN = (((0,), (0,)), ((), ()))


def _dot(a, b):
    return jnp.dot(a, b, preferred_element_type=F32)


def _dot_nt(a, b):
    return lax.dot_general(a, b, _NT, preferred_element_type=F32)


def _dot_tn(a, b):
    return lax.dot_general(a, b, _TN, preferred_element_type=F32)


def _split_bf16(a):
    hi = a.astype(BF16)
    lo = (a - hi.astype(F32)).astype(BF16)
    return hi, lo


def _rms(x, g):
    r = lax.rsqrt(jnp.mean(x * x, axis=-1, keepdims=True) + EPS)
    return (x * r) * g


def _params(*sem):
    return pltpu.CompilerParams(dimension_semantics=sem, vmem_limit_bytes=VMEM_LIMIT)


def _resident(shape):
    nd = len(shape)
    return pl.BlockSpec(shape, lambda *_: (0,) * nd, pipeline_mode=pl.Buffered(1))


def _layer(arr, l):
    nd = arr.ndim
    return pl.BlockSpec((None,) + arr.shape[1:], lambda *_: (l,) + (0,) * (nd - 1),
                        pipeline_mode=pl.Buffered(1))


def _norm_inproj_kernel(x_ref, g_ref, wt_ref, *refs, scales, tail):
    out_refs, wb_ref = refs[:-1], refs[-1]
    in_w = wt_ref.shape[0]

    @pl.when(pl.program_id(0) == 0)
    def _prepare():
        off = 0
        for o_ref, scale in zip(out_refs[:-1], scales):
            wd = o_ref.shape[0] * o_ref.shape[2] if len(o_ref.shape) == 3 else o_ref.shape[-1]
            for c0 in range(off, off + wd, WT_CHUNK):
                blk = wt_ref[c0:c0 + WT_CHUNK, :]
                if scale != 1.0:
                    blk = blk * scale
                wb_ref[:, c0:c0 + WT_CHUNK] = blk.T.astype(BF16)
            off += wd
        wb_ref[:, off:off + tail] = wt_ref[in_w - tail:in_w, :].T.astype(BF16)

    h = _rms(x_ref[...], g_ref[...]).astype(BF16)
    off = 0
    for o_ref in out_refs:
        pair_major = len(o_ref.shape) == 3
        wd = o_ref.shape[0] * o_ref.shape[2] if pair_major else o_ref.shape[-1]
        res = _dot(h, wb_ref[:, off:off + wd]).astype(o_ref.dtype)
        if pair_major:
            for p in range(o_ref.shape[0]):
                o_ref[p] = res[:, p * o_ref.shape[2]:(p + 1) * o_ref.shape[2]]
        else:
            o_ref[...] = res
        off += wd


def _norm_inproj(x, g, wt, l, widths, pair_major, scales, tm):
    n, d = x.shape
    assert all(wd % WT_CHUNK == 0 for wd in widths[:-1]) and sum(widths[:-1]) <= wt.shape[1]
    specs, shapes = [], []
    for wd, pm in zip(widths, pair_major):
        if pm:
            specs.append(pl.BlockSpec((wd // LANES, tm, LANES), lambda i: (0, i, 0)))
            shapes.append(jax.ShapeDtypeStruct((wd // LANES, n, LANES), BF16))
        else:
            specs.append(pl.BlockSpec((tm, wd), lambda i: (i, 0)))
            shapes.append(jax.ShapeDtypeStruct((n, wd), BF16))
    return pl.pallas_call(
        functools.partial(_norm_inproj_kernel, scales=scales, tail=widths[-1]),
        grid=(n // tm,),
        in_specs=[pl.BlockSpec((tm, d), lambda i: (i, 0)), _layer(g, l), _layer(wt, l)],
        out_specs=specs,
        out_shape=shapes,
        scratch_shapes=[pltpu.VMEM((d, sum(widths)), BF16)],
        compiler_params=_params("arbitrary"),
        name="norm_inproj",
    )(x, g, wt)


def _moba_kernel(q_ref, k_ref, v_ref, kaux_ref, s256_ref, o_ref,
                 kaug_ref, vt_ref, kmean_ref, s_ref, p_ref, acc_ref, *, nblk, blk):
    j = pl.program_id(2)
    hd2 = 2 * HEAD_DIM
    tq = 2 * blk
    kt = 2 * blk

    @pl.when(j == 0)
    def _prepare():
        lane = lax.broadcasted_iota(jnp.int32, (blk, hd2), 1)
        aux = kaux_ref[0].astype(F32)
        ones = jnp.ones((VT_ROWS - HEAD_DIM, blk), BF16)
        for n in range(nblk):
            rows = slice(n * blk, (n + 1) * blk)
            kb = k_ref[rows, :]
            kaug_ref[rows, 0:hd2] = kb
            kaug_ref[rows, hd2:2 * hd2] = (aux + jnp.where(lane == n, 1.0, 0.0)).astype(BF16)
            vt = v_ref[rows, :].astype(F32).T.astype(BF16)
            for hh in range(2):
                vt_ref[hh, 0:HEAD_DIM, rows] = vt[hh * HEAD_DIM:(hh + 1) * HEAD_DIM]
                vt_ref[hh, HEAD_DIM:VT_ROWS, rows] = ones
            kmean_ref[n:n + 1, :] = jnp.sum(kb.astype(F32), axis=0, keepdims=True) * (1.0 / blk)

    qt = q_ref[...].astype(F32).T
    qt_b = qt.astype(BF16)
    km = kmean_ref[...]
    km_lane_head = lax.broadcasted_iota(jnp.int32, (nblk, hd2), 1) >> LOG2_HEAD_DIM
    qt_row_head = lax.broadcasted_iota(jnp.int32, (hd2, tq), 0) >> LOG2_HEAD_DIM
    nidx = lax.broadcasted_iota(jnp.int32, (nblk, tq), 0)
    qblk = 2 * j + (lax.broadcasted_iota(jnp.int32, (nblk, tq), 1) >> LOG2_MOBA_BLOCK)
    aux_row = lax.broadcasted_iota(jnp.int32, (hd2 - nblk, tq), 0)
    past = nidx < qblk

    qaug = []
    for hh in range(2):
        kmh = jnp.where(km_lane_head == hh, km, 0.0)
        kh, kl = _split_bf16(kmh)
        gate = _dot(kh, qt_b) + _dot(kl, qt_b)
        g = jnp.where(past, gate, -jnp.inf)
        rank = jnp.zeros((nblk, tq), F32)
        for m in range(nblk):
            gm = g[m:m + 1, :]
            rank = rank + jnp.where(m < nidx, jnp.where(gm >= g, 1.0, 0.0), jnp.where(gm > g, 1.0, 0.0))
        sel = (past & (rank < MOBA_TOPK)) | (nidx == qblk)
        s256 = s256_ref[0, hh:hh + 1, :]
        bias = jnp.where(sel, (nidx - qblk).astype(F32) * s256, MASK_BIAS)
        head_one = jnp.where(aux_row == hh, 1.0, 0.0)
        qaug.append(jnp.concatenate(
            [jnp.where(qt_row_head == hh, qt, 0.0), bias, head_one], axis=0).astype(BF16))
    qall = jnp.concatenate(qaug, axis=1)

    ncg = 4
    cg = [slice(c * blk, (c + 1) * blk) for c in range(ncg)]

    def softmax_c(st, m, c):
        mn = jnp.maximum(m, jnp.max(st, axis=0, keepdims=True))
        p_ref[:, cg[c]] = jnp.exp(st - mn).astype(BF16)
        return mn, jnp.exp(m - mn)

    def accumulate_c(n, alpha, c):
        off = pl.multiple_of(n * kt, kt)
        hh, a = c // 2, c % 2
        acc_ref[hh, :, a * blk:(a + 1) * blk] = (
            alpha * acc_ref[hh, :, a * blk:(a + 1) * blk]
            + _dot(vt_ref[hh, :, pl.ds(off, kt)], p_ref[:, cg[c]]))

    s_ref[...] = _dot(kaug_ref[0:kt, :], qall)
    p_ref[...] = jnp.zeros_like(p_ref)
    acc_ref[...] = jnp.zeros_like(acc_ref)
    m0 = tuple(jnp.full((1, blk), -jnp.inf, F32) for _ in range(ncg))
    a0 = tuple(jnp.ones((1, blk), F32) for _ in range(ncg))

    def body(i, carry):
        ms, alphas = carry
        off = pl.multiple_of((i + 1) * kt, kt)
        ktile = kaug_ref[pl.ds(off, kt), :]
        new_m, new_a, nxt = [], [], []
        for c in range(ncg):
            nxt.append(_dot(ktile, qall[:, cg[c]]))
            accumulate_c(jnp.maximum(i - 1, 0), alphas[c], c)
            mn, al = softmax_c(s_ref[:, cg[c]], ms[c], c)
            new_m.append(mn)
            new_a.append(al)
        for c in range(ncg):
            s_ref[:, cg[c]] = nxt[c]
        return tuple(new_m), tuple(new_a)

    ms, alphas = lax.fori_loop(0, j, body, (m0, a0))
    for c in range(ncg):
        accumulate_c(jnp.maximum(j - 1, 0), alphas[c], c)
    key_i = lax.broadcasted_iota(jnp.int32, (blk, blk), 0)
    qry_i = lax.broadcasted_iota(jnp.int32, (blk, blk), 1)
    causal = key_i <= qry_i
    off = pl.multiple_of(j * kt, kt)
    for c in range(ncg):
        hh, a = c // 2, c % 2
        parts = [jnp.where(causal, s_ref[0:blk, cg[c]], -jnp.inf)] if a == 0 else [
            s_ref[0:blk, cg[c]], jnp.where(causal, s_ref[blk:kt, cg[c]], -jnp.inf)]
        mn = ms[c]
        for st in parts:
            mn = jnp.maximum(mn, jnp.max(st, axis=0, keepdims=True))
        for r, st in enumerate(parts):
            p_ref[r * blk:(r + 1) * blk, cg[c]] = jnp.exp(st - mn).astype(BF16)
        nk = len(parts) * blk
        acc_ref[hh, :, a * blk:(a + 1) * blk] = (
            jnp.exp(ms[c] - mn) * acc_ref[hh, :, a * blk:(a + 1) * blk]
            + _dot(vt_ref[hh, :, pl.ds(off, nk)], p_ref[0:nk, cg[c]]))

    for a in range(2):
        ot = []
        for hh in range(2):
            acc = acc_ref[hh, :, a * blk:(a + 1) * blk]
            ot.append(acc[0:HEAD_DIM] * (1.0 / acc[HEAD_DIM:HEAD_DIM + 1]))
        o_ref[a * blk:(a + 1) * blk, :] = jnp.concatenate(ot, axis=0).T.astype(o_ref.dtype)


def _moba_tables(nblk):
    slopes = 2.0 ** (-8.0 * np.arange(1, N_MOBA_HEADS + 1) / N_MOBA_HEADS)
    npair = N_MOBA_HEADS // 2
    kaux = np.zeros((npair, MOBA_BLOCK, 2 * HEAD_DIM), np.float32)
    s256 = np.zeros((npair, 8, 2 * MOBA_BLOCK), np.float32)
    for p in range(npair):
        for hh in range(2):
            kaux[p, :, nblk + hh] = slopes[2 * p + hh] * np.arange(MOBA_BLOCK)
            s256[p, hh, :] = slopes[2 * p + hh] * MOBA_BLOCK
    return jnp.asarray(kaux, BF16), jnp.asarray(s256, F32)


def _moba(q, k, v, bsz):
    npair, n, hd2 = q.shape
    t = n // bsz
    blk = MOBA_BLOCK
    nblk = t // blk
    assert blk == 1 << LOG2_MOBA_BLOCK and t % (2 * blk) == 0 and nblk % 8 == 0 and nblk <= HEAD_DIM
    assert hd2 == 2 * HEAD_DIM
    tq = 2 * blk
    steps = t // tq
    kaux, s256 = _moba_tables(nblk)
    q_spec = pl.BlockSpec((None, tq, hd2), lambda bi, p, i: (p, bi * steps + i, 0))
    kv_spec = pl.BlockSpec((None, t, hd2), lambda bi, p, i: (p, bi, 0))
    return pl.pallas_call(
        functools.partial(_moba_kernel, nblk=nblk, blk=blk),
        grid=(bsz, npair, steps),
        in_specs=[q_spec, kv_spec, kv_spec,
                  pl.BlockSpec((1, blk, hd2), lambda bi, p, i: (p, 0, 0)),
                  pl.BlockSpec((1, 8, tq), lambda bi, p, i: (p, 0, 0))],
        out_specs=q_spec,
        out_shape=jax.ShapeDtypeStruct((npair, n, hd2), BF16),
        scratch_shapes=[pltpu.VMEM((t, 2 * hd2), BF16),
                        pltpu.VMEM((2, VT_ROWS, t), BF16),
                        pltpu.VMEM((nblk, hd2), F32),
                        pltpu.VMEM((2 * blk, 2 * tq), F32),
                        pltpu.VMEM((2 * blk, 2 * tq), BF16),
                        pltpu.VMEM((2, VT_ROWS, tq), F32)],
        compiler_params=_params("parallel", "parallel", "arbitrary"),
        name="moba",
    )(q, k, v, kaux, s256)


def _gla_kernel(qk_ref, v_ref, go_ref, gl_ref, wg_ref, bg_ref, gn_ref, o_ref, st_ref,
                *, sub, chunk):
    kw = N_GLA_HEADS * GLA_DK
    vw = N_GLA_HEADS * GLA_DV
    tt = qk_ref.shape[1]

    @pl.when(pl.program_id(1) == 0)
    def _reset():
        st_ref[...] = jnp.zeros_like(st_ref)

    r = lax.broadcasted_iota(jnp.int32, (sub, sub), 0)
    c = lax.broadcasted_iota(jnp.int32, (sub, sub), 1)
    same = (r >> LOG2_HEAD_DIM) == (c >> LOG2_HEAD_DIM)
    causal = same & (c <= r)
    ltri = jnp.where(causal, 1.0, 0.0).astype(BF16)
    st_diag = ((lax.broadcasted_iota(jnp.int32, (vw, kw), 0) >> LOG2_HEAD_DIM)
               == (lax.broadcasted_iota(jnp.int32, (vw, kw), 1) >> LOG2_GLA_DK))
    grp = jnp.where((lax.broadcasted_iota(jnp.int32, (vw, vw), 0) >> LOG2_HEAD_DIM)
                    == (lax.broadcasted_iota(jnp.int32, (vw, vw), 1) >> LOG2_HEAD_DIM),
                    1.0, 0.0).astype(BF16)
    khead = lax.broadcasted_iota(jnp.int32, (sub, kw), 1) >> LOG2_GLA_DK
    vhead = lax.broadcasted_iota(jnp.int32, (sub, vw), 1) >> LOG2_HEAD_DIM

    for s in range(tt // sub):
        rows = slice(s * sub, (s + 1) * sub)
        z = _dot(gl_ref[0, rows, :], wg_ref[...]) + bg_ref[...]
        la = (jnp.minimum(z, 0.0) - jnp.log(1.0 + jnp.exp(-jnp.abs(z)))) * (1.0 / GLA_GATE_NORMALIZER)
        la_hi, la_lo = _split_bf16(la)
        b = _dot(ltri, la_hi) + _dot(ltri, la_lo)
        tot = [b[(ci + 1) * chunk - 1:(ci + 1) * chunk, :] for ci in range(sub // chunk)]
        bl = jnp.concatenate([jnp.broadcast_to(t, (chunk, kw)) for t in tot], axis=0)
        q = qk_ref[0, rows, 0:kw].astype(F32)
        k = qk_ref[0, rows, kw:2 * kw].astype(F32)
        v = v_ref[0, rows, :]
        qd = (q * (jnp.exp(b) * (GLA_DK ** -0.5))).astype(BF16)
        kd = (k * jnp.exp(-b)).astype(BF16)
        kend = (k * jnp.exp(bl - b)).astype(BF16)

        o = jnp.zeros((sub, vw), F32)
        for h in range(N_GLA_HEADS):
            a = _dot_nt(jnp.where(khead == h, qd, jnp.zeros_like(qd)), kd)
            a = jnp.where(causal, a, 0.0).astype(BF16)
            o = o + _dot(a, jnp.where(vhead == h, v, jnp.zeros_like(v)))

        inter = []
        for ci in range(sub // chunk):
            cr = slice(ci * chunk, (ci + 1) * chunk)
            st = st_ref[...]
            inter.append(_dot_nt(qd[cr], st.astype(BF16)))
            ut = _dot_tn(v[cr], kend[cr])
            st_ref[...] = st * jnp.exp(tot[ci]) + jnp.where(st_diag, ut, 0.0)
        o = o + jnp.concatenate(inter, axis=0)

        o2_hi, o2_lo = _split_bf16(o * o)
        ms = (_dot(o2_hi, grp) + _dot(o2_lo, grp)) * (1.0 / GLA_DV)
        go = go_ref[0, rows, :].astype(F32)
        y = (o * lax.rsqrt(ms + EPS)) * gn_ref[...] * (go * (1.0 / (1.0 + jnp.exp(-go))))
        o_ref[0, rows, :] = y.astype(o_ref.dtype)


def _gla(qk, v, go, gl, wg, bg, gn, l, tt):
    b, t, _ = qk.shape
    kw = N_GLA_HEADS * GLA_DK
    vw = N_GLA_HEADS * GLA_DV
    spec = lambda w: pl.BlockSpec((1, tt, w), lambda bi, i: (bi, i, 0))
    return pl.pallas_call(
        functools.partial(_gla_kernel, sub=4 * GLA_CHUNK, chunk=GLA_CHUNK),
        grid=(b, t // tt),
        in_specs=[spec(2 * kw), spec(vw), spec(vw), spec(LANES),
                  _layer(wg, l), _layer(bg, l), _layer(gn, l)],
        out_specs=spec(vw),
        out_shape=jax.ShapeDtypeStruct((b, t, vw), BF16),
        scratch_shapes=[pltpu.VMEM((vw, kw), F32)],
        compiler_params=_params("parallel", "arbitrary"),
        name="gla",
    )(qk, v, go, gl, wg, bg, gn)


def _mem_kv_kernel(mem_ref, g_ref, w_ref, k_ref, v_ref):
    h = _rms(mem_ref[0], g_ref[0]).astype(BF16)
    kv = _dot(h, w_ref[0])
    wd = k_ref.shape[-1]
    k_ref[0, 0] = kv[:, :wd].astype(BF16)
    v_ref[0, 0] = kv[:, wd:].astype(BF16)


def _mem_kv(mem, g, wkv):
    b, m, d = mem.shape
    nl = wkv.shape[0]
    wd = wkv.shape[-1] // 2
    out = jax.ShapeDtypeStruct((nl, b, m, wd), BF16)
    ospec = pl.BlockSpec((1, 1, m, wd), lambda l, bi: (l, bi, 0, 0))
    return pl.pallas_call(
        _mem_kv_kernel,
        grid=(nl, b),
        in_specs=[pl.BlockSpec((1, m, d), lambda l, bi: (bi, 0, 0)),
                  pl.BlockSpec((1, 1, d), lambda l, bi: (l, 0, 0)),
                  pl.BlockSpec((1, d, 2 * wd), lambda l, bi: (l, 0, 0))],
        out_specs=[ospec, ospec],
        out_shape=[out, out],
        compiler_params=_params("parallel", "parallel"),
        name="mem_kv",
    )(mem, g, wkv)


def _mix_xattn_kernel(x_ref, c3_ref, c3h_ref, ym_ref, yg_ref, wout_f32, cw_ref, cb_ref,
                      gx_ref, wq_f32, mk_ref, mv_ref, wo_f32, o_ref, wout_ref, wq_ref, wo_ref,
                      *, tiles_per_seq):
    tm = x_ref.shape[0]

    @pl.when(pl.program_id(0) == 0)
    def _cast_weights():
        wout_ref[...] = wout_f32[...].astype(BF16)
        wq_ref[...] = (wq_f32[...] * (HEAD_DIM ** -0.5)).astype(BF16)
        wo_ref[...] = wo_f32[...].astype(BF16)

    cc = cw_ref.shape[-1]
    nh = N_XATTN_HEADS
    first = (pl.program_id(0) % tiles_per_seq) == 0

    c3 = c3_ref[...].astype(F32)
    u = c3[:, cc:2 * cc] * c3[:, 2 * cc:3 * cc]
    halo = c3h_ref[...].astype(F32)
    uh = halo[:, cc:2 * cc] * halo[:, 2 * cc:3 * cc]
    uh = jnp.where(first, 0.0, uh)
    last1 = uh[BF16_SUBLANES - 1:BF16_SUBLANES, :]
    last2 = uh[BF16_SUBLANES - 2:BF16_SUBLANES - 1, :]
    row = lax.broadcasted_iota(jnp.int32, (tm, cc), 0)
    u1 = jnp.where(row == 0, last1, pltpu.roll(u, 1, 0))
    u2 = jnp.where(row == 0, last2, jnp.where(row == 1, last1, pltpu.roll(u, 2, 0)))
    cw = cw_ref[...]
    yc = c3[:, 0:cc] * (cw[0:1] * u2 + cw[1:2] * u1 + cw[2:3] * u + cb_ref[...])

    y = jnp.concatenate([yc.astype(BF16)] + [ym_ref[p] for p in range(ym_ref.shape[0])]
                        + [yg_ref[...]], axis=1)
    x1 = x_ref[...] + _dot(y, wout_ref[...])

    h = _rms(x1, gx_ref[...]).astype(BF16)
    q = _dot(h, wq_ref[...]).astype(BF16)
    k = mk_ref[...]
    v = mv_ref[...]
    qhead = lax.broadcasted_iota(jnp.int32, q.shape, 1) >> LOG2_HEAD_DIM
    vhead = lax.broadcasted_iota(jnp.int32, v.shape, 1) >> LOG2_HEAD_DIM
    o = jnp.zeros(q.shape, F32)
    for hd in range(nh):
        s = _dot_nt(jnp.where(qhead == hd, q, jnp.zeros_like(q)), k)
        p = jnp.exp(s - jnp.max(s, axis=-1, keepdims=True))
        l = jnp.sum(p, axis=-1, keepdims=True)
        pv = _dot(p.astype(BF16), jnp.where(vhead == hd, v, jnp.zeros_like(v)))
        o = o + pv * (1.0 / l)
    o_ref[...] = x1 + _dot(o.astype(BF16), wo_ref[...])


def _mix_xattn(x, c3, ym, yg, wout, cw, cb, gx, wq, memk, memv, wo, l, *, tm, seq):
    n, d = x.shape
    tiles_per_seq = seq // tm
    hb = tm // BF16_SUBLANES
    row = lambda w: pl.BlockSpec((tm, w), lambda i: (i, 0))
    mem_spec = pl.BlockSpec((None, None) + memk.shape[2:], lambda i: (l, i // tiles_per_seq, 0, 0))
    return pl.pallas_call(
        functools.partial(_mix_xattn_kernel, tiles_per_seq=tiles_per_seq),
        grid=(n // tm,),
        in_specs=[row(d), row(c3.shape[1]),
                  pl.BlockSpec((BF16_SUBLANES, c3.shape[1]),
                               lambda i: (jnp.maximum(i * hb - 1, 0), 0)),
                  pl.BlockSpec((ym.shape[0], tm, ym.shape[2]), lambda i: (0, i, 0)),
                  row(yg.shape[1]),
                  _layer(wout, l), _layer(cw, l), _layer(cb, l), _layer(gx, l), _layer(wq, l),
                  mem_spec, mem_spec, _layer(wo, l)],
        out_specs=row(d),
        out_shape=jax.ShapeDtypeStruct((n, d), F32),
        scratch_shapes=[pltpu.VMEM(wout.shape[1:], BF16), pltpu.VMEM(wq.shape[1:], BF16),
                        pltpu.VMEM(wo.shape[1:], BF16)],
        compiler_params=_params("arbitrary"),
        name="mix_xattn",
    )(x, c3, c3, ym, yg, wout, cw, cb, gx, wq, memk, memv, wo)


def _ffn_kernel(x_ref, g_ref, wa_ref, wg_ref, cwa_ref, cwg_ref, cba_ref, cbg_ref, wd_ref,
                gf_ref, o_ref, hext_ref, hid_ref, ua_ref, ug_ref, *, tiles_per_seq, final):
    tm = x_ref.shape[0]
    halo = BF16_SUBLANES
    tn = ua_ref.shape[1]
    nc = wa_ref.shape[1] // tn
    first = (pl.program_id(0) % tiles_per_seq) == 0

    def cols(j):
        return pl.ds(pl.multiple_of(j * tn, tn), tn)

    @pl.when(first)
    def _zero_halo():
        hext_ref[0:halo, :] = jnp.zeros((halo, hext_ref.shape[1]), BF16)

    @pl.when(jnp.logical_not(first))
    def _carry_halo():
        hext_ref[0:halo, :] = hext_ref[tm:tm + halo, :]

    x = x_ref[...]
    hext_ref[halo:, :] = _rms(x, g_ref[...]).astype(BF16)

    def up(j):
        hext = hext_ref[...]
        return _dot(hext, wa_ref[:, cols(j)]), _dot(hext, wg_ref[:, cols(j)])

    def conv(u, cw, cb):
        y = cw[0:1] * pltpu.roll(u, 2, 0) + cw[1:2] * pltpu.roll(u, 1, 0) + cw[2:3] * u + cb
        return y[F32_SUBLANES:]

    def gate(j):
        src = slice(halo - F32_SUBLANES, halo + tm)
        a = conv(ua_ref[src, :], cwa_ref[:, cols(j)], cba_ref[:, cols(j)])
        g = conv(ug_ref[src, :], cwg_ref[:, cols(j)], cbg_ref[:, cols(j)])
        h2 = (0.5 * a).astype(BF16)
        hid_ref[:, cols(j)] = (h2 + h2 * jnp.tanh(h2)) * g.astype(BF16)

    ua_ref[...], ug_ref[...] = up(0)

    def body(j, carry):
        ua, ug = up(j + 1)
        gate(j)
        ua_ref[...] = ua
        ug_ref[...] = ug
        return carry

    lax.fori_loop(0, nc - 1, body, 0)
    gate(nc - 1)
    y = x + _dot(hid_ref[...], wd_ref[...])
    if final:
        y = _rms(y, gf_ref[...])
    o_ref[...] = y


def _ffn(x, g, wa, wg, cwa, cwg, cba, cbg, wd, gf, l, *, tm, tn, seq, final):
    n, d = x.shape
    return pl.pallas_call(
        functools.partial(_ffn_kernel, tiles_per_seq=seq // tm, final=final),
        grid=(n // tm,),
        in_specs=[pl.BlockSpec((tm, d), lambda i: (i, 0)), _layer(g, l),
                  _layer(wa, l), _layer(wg, l), _layer(cwa, l), _layer(cwg, l),
                  _layer(cba, l), _layer(cbg, l), _layer(wd, l), _resident(gf.shape)],
        out_specs=pl.BlockSpec((tm, d), lambda i: (i, 0)),
        out_shape=jax.ShapeDtypeStruct((n, d), F32),
        scratch_shapes=[pltpu.VMEM((tm + BF16_SUBLANES, d), BF16),
                        pltpu.VMEM((tm, wa.shape[-1]), BF16),
                        pltpu.VMEM((tm + BF16_SUBLANES, tn), F32),
                        pltpu.VMEM((tm + BF16_SUBLANES, tn), F32)],
        compiler_params=_params("arbitrary"),
        name="ffn",
    )(x, g, wa, wg, cwa, cwg, cba, cbg, wd, gf)


def _pad_cols(w, pad_to):
    return jnp.pad(w, [(0, 0)] * (w.ndim - 1) + [(0, pad_to - w.shape[-1])])


def _split_up_kernel(w_ref, a_ref, g_ref, *, d_ff):
    w = w_ref[...]
    zeros = jnp.zeros((w.shape[0], a_ref.shape[-1] - d_ff), BF16)
    a_ref[:, 0:d_ff] = w[:, 0:d_ff].astype(BF16)
    a_ref[:, d_ff:] = zeros
    g_ref[:, 0:d_ff] = w[:, d_ff:2 * d_ff].astype(BF16)
    g_ref[:, d_ff:] = zeros


def _split_up(w, d_ff, ff_pad, rows):
    nl, d, _ = w.shape
    out = jax.ShapeDtypeStruct((nl, d, ff_pad), BF16)
    ospec = pl.BlockSpec((None, rows, ff_pad), lambda l, i: (l, i, 0))
    return pl.pallas_call(
        functools.partial(_split_up_kernel, d_ff=d_ff),
        grid=(nl, d // rows),
        in_specs=[pl.BlockSpec((None, rows, 2 * d_ff), lambda l, i: (l, i, 0))],
        out_specs=[ospec, ospec],
        out_shape=[out, out],
        compiler_params=_params("parallel", "parallel"),
        name="ffn_up_prep",
    )(w)


def _pad_down_kernel(w_ref, o_ref):
    rows = w_ref.shape[0]
    o_ref[0:rows, :] = w_ref[...].astype(BF16)
    o_ref[rows:, :] = jnp.zeros((o_ref.shape[0] - rows, o_ref.shape[1]), BF16)


def _pad_down(w, ff_pad):
    nl, d_ff, d = w.shape
    return pl.pallas_call(
        _pad_down_kernel,
        grid=(nl,),
        in_specs=[pl.BlockSpec((None, d_ff, d), lambda l: (l, 0, 0))],
        out_specs=pl.BlockSpec((None, ff_pad, d), lambda l: (l, 0, 0)),
        out_shape=jax.ShapeDtypeStruct((nl, ff_pad, d), BF16),
        compiler_params=_params("parallel"),
        name="ffn_down_prep",
    )(w)


def kernel(x, mem, norm_mix_g, w_in, sc_conv_w, sc_conv_b, gla_w_gate, gla_b_gate, gla_norm_g,
           w_out, norm_xattn_g, norm_mem_g, xattn_wq, xattn_wkv, xattn_wo, norm_ffn_g,
           ffn_w_up, ffn_conv_w, ffn_conv_b, ffn_w_down, final_norm_g):
    bsz, seq, d = x.shape
    nl = w_in.shape[0]
    n = bsz * seq
    conv_ch = sc_conv_w.shape[-1]
    moba_w = N_MOBA_HEADS * HEAD_DIM
    gk_w = N_GLA_HEADS * GLA_DK
    gv_w = N_GLA_HEADS * GLA_DV
    d_ff = ffn_w_down.shape[1]
    ff_pad = -(-d_ff // FFN_CHUNK) * FFN_CHUNK
    tm = TOKEN_TILE

    widths = (3 * conv_ch, moba_w, moba_w, moba_w, 2 * gk_w, gv_w, gv_w, LANES)
    pair_major = (False, True, True, True, False, False, False, False)
    in_scales = (1.0, HEAD_DIM ** -0.5, 1.0, 1.0, 1.0, 1.0, 1.0)
    w_in_t = jnp.swapaxes(w_in, 1, 2)
    wg_p = jnp.pad(gla_w_gate.astype(BF16), ((0, 0), (LANES - GLA_GATE_RANK, 0), (0, 0)))
    gn_t = jnp.tile(gla_norm_g, (1, N_GLA_HEADS))
    wkv_b = xattn_wkv.astype(BF16)
    wa, wgt = _split_up(ffn_w_up, d_ff, ff_pad, rows=FFN_CHUNK)
    cwa = _pad_cols(ffn_conv_w[..., :d_ff], ff_pad)
    cwg = _pad_cols(ffn_conv_w[..., d_ff:], ff_pad)
    cba = _pad_cols(ffn_conv_b[:, None, :d_ff], ff_pad)
    cbg = _pad_cols(ffn_conv_b[:, None, d_ff:], ff_pad)
    wd = _pad_down(ffn_w_down, ff_pad)

    memk, memv = _mem_kv(mem, norm_mem_g[:, None, :], wkv_b)

    vec = lambda a: a[:, None, :]
    norm_mix_v, norm_xattn_v, norm_ffn_v = vec(norm_mix_g), vec(norm_xattn_g), vec(norm_ffn_g)
    gla_bg_v, gn_v, sc_conv_b_v = vec(gla_b_gate), vec(gn_t), vec(sc_conv_b)

    xf = x.reshape(n, d)
    for l in range(nl):
        c3, mq, mk, mv, gqk, gv, go, gl = _norm_inproj(
            xf, norm_mix_v, w_in_t, l, widths, pair_major, in_scales, tm)
        r3 = lambda a: a.reshape(bsz, seq, a.shape[-1])
        ym = _moba(mq, mk, mv, bsz)
        yg = _gla(r3(gqk), r3(gv), r3(go), r3(gl), wg_p, gla_bg_v, gn_v, l,
                  tt=GLA_TILE).reshape(n, gv_w)
        xf = _mix_xattn(xf, c3, ym, yg, w_out, sc_conv_w, sc_conv_b_v, norm_xattn_v, xattn_wq,
                        memk, memv, xattn_wo, l, tm=tm, seq=seq)
        xf = _ffn(xf, norm_ffn_v, wa, wgt, cwa, cwg, cba, cbg, wd, final_norm_g[None, :], l,
                  tm=tm, tn=FFN_CHUNK, seq=seq, final=(l == nl - 1))
    return xf.reshape(bsz, seq, d)
```

```python
import functools

import numpy as np
import jax
import jax.numpy as jnp
from jax import lax
from jax.experimental import pallas as pl
from jax.experimental.pallas import tpu as pltpu

F32 = jnp.float32
BF16 = jnp.bfloat16

EPS = 1e-6
HEAD_DIM = 64
N_MOBA_HEADS = 8
MOBA_BLOCK = 256
MOBA_TOPK = 3
N_GLA_HEADS = 4
GLA_DK = 32
GLA_DV = 64
GLA_CHUNK = 64
GLA_GATE_RANK = 16
GLA_GATE_NORMALIZER = 16.0
CONV_K = 3
N_XATTN_HEADS = 4
LOG2_HEAD_DIM = 6
LOG2_GLA_DK = 5
LOG2_MOBA_BLOCK = 8

LANES = 128
F32_SUBLANES = 8
BF16_SUBLANES = 16
VMEM_LIMIT = 56 * 1024 * 1024
TOKEN_TILE = 1024
FFN_CHUNK = 256
GLA_TILE = 1024
WT_CHUNK = 256

VT_ROWS = HEAD_DIM + 16
MASK_BIAS = -(2.0 ** 60)

_NT = (((1,), (1,)), ((), ()))
_TN = (((0,), (0,)), ((), ()))


def _dot(a, b):
    return jnp.dot(a, b, preferred_element_type=F32)


def _dot_nt(a, b):
    return lax.dot_general(a, b, _NT, preferred_element_type=F32)


def _dot_tn(a, b):
    return lax.dot_general(a, b, _TN, preferred_element_type=F32)


def _split_bf16(a):
    hi = a.astype(BF16)
    lo = (a - hi.astype(F32)).astype(BF16)
    return hi, lo


def _rms(x, g):
    r = lax.rsqrt(jnp.mean(x * x, axis=-1, keepdims=True) + EPS)
    return (x * r) * g


def _params(*sem):
    return pltpu.CompilerParams(dimension_semantics=sem, vmem_limit_bytes=VMEM_LIMIT)


def _resident(shape):
    nd = len(shape)
    return pl.BlockSpec(shape, lambda *_: (0,) * nd, pipeline_mode=pl.Buffered(1))


def _layer(arr, l):
    nd = arr.ndim
    return pl.BlockSpec((None,) + arr.shape[1:], lambda *_: (l,) + (0,) * (nd - 1),
                        pipeline_mode=pl.Buffered(1))


def _norm_inproj_kernel(x_ref, g_ref, wt_ref, *refs, scales, tail):
    out_refs, wb_ref = refs[:-1], refs[-1]
    in_w = wt_ref.shape[0]

    @pl.when(pl.program_id(0) == 0)
    def _prepare():
        off = 0
        for o_ref, scale in zip(out_refs[:-1], scales):
            wd = o_ref.shape[0] * o_ref.shape[2] if len(o_ref.shape) == 3 else o_ref.shape[-1]
            for c0 in range(off, off + wd, WT_CHUNK):
                blk = wt_ref[c0:c0 + WT_CHUNK, :]
                if scale != 1.0:
                    blk = blk * scale
                wb_ref[:, c0:c0 + WT_CHUNK] = blk.T.astype(BF16)
            off += wd
        wb_ref[:, off:off + tail] = wt_ref[in_w - tail:in_w, :].T.astype(BF16)

    h = _rms(x_ref[...], g_ref[...]).astype(BF16)
    off = 0
    for o_ref in out_refs:
        pair_major = len(o_ref.shape) == 3
        wd = o_ref.shape[0] * o_ref.shape[2] if pair_major else o_ref.shape[-1]
        res = _dot(h, wb_ref[:, off:off + wd]).astype(o_ref.dtype)
        if pair_major:
            for p in range(o_ref.shape[0]):
                o_ref[p] = res[:, p * o_ref.shape[2]:(p + 1) * o_ref.shape[2]]
        else:
            o_ref[...] = res
        off += wd


def _norm_inproj(x, g, wt, l, widths, pair_major, scales, tm):
    n, d = x.shape
    assert all(wd % WT_CHUNK == 0 for wd in widths[:-1]) and sum(widths[:-1]) <= wt.shape[1]
    specs, shapes = [], []
    for wd, pm in zip(widths, pair_major):
        if pm:
            specs.append(pl.BlockSpec((wd // LANES, tm, LANES), lambda i: (0, i, 0)))
            shapes.append(jax.ShapeDtypeStruct((wd // LANES, n, LANES), BF16))
        else:
            specs.append(pl.BlockSpec((tm, wd), lambda i: (i, 0)))
            shapes.append(jax.ShapeDtypeStruct((n, wd), BF16))
    return pl.pallas_call(
        functools.partial(_norm_inproj_kernel, scales=scales, tail=widths[-1]),
        grid=(n // tm,),
        in_specs=[pl.BlockSpec((tm, d), lambda i: (i, 0)), _layer(g, l), _layer(wt, l)],
        out_specs=specs,
        out_shape=shapes,
        scratch_shapes=[pltpu.VMEM((d, sum(widths)), BF16)],
        compiler_params=_params("arbitrary"),
        name="norm_inproj",
    )(x, g, wt)


def _moba_kernel(q_ref, k_ref, v_ref, kaux_ref, s256_ref, o_ref,
                 kaug_ref, vt_ref, kmean_ref, s_ref, p_ref, acc_ref, *, nblk, blk):
    j = pl.program_id(2)
    hd2 = 2 * HEAD_DIM
    tq = 2 * blk
    kt = 2 * blk

    @pl.when(j == 0)
    def _prepare():
        lane = lax.broadcasted_iota(jnp.int32, (blk, hd2), 1)
        aux = kaux_ref[0].astype(F32)
        ones = jnp.ones((VT_ROWS - HEAD_DIM, blk), BF16)
        for n in range(nblk):
            rows = slice(n * blk, (n + 1) * blk)
            kb = k_ref[rows, :]
            kaug_ref[rows, 0:hd2] = kb
            kaug_ref[rows, hd2:2 * hd2] = (aux + jnp.where(lane == n, 1.0, 0.0)).astype(BF16)
            vt = v_ref[rows, :].astype(F32).T.astype(BF16)
            for hh in range(2):
                vt_ref[hh, 0:HEAD_DIM, rows] = vt[hh * HEAD_DIM:(hh + 1) * HEAD_DIM]
                vt_ref[hh, HEAD_DIM:VT_ROWS, rows] = ones
            kmean_ref[n:n + 1, :] = jnp.sum(kb.astype(F32), axis=0, keepdims=True) * (1.0 / blk)
        p_ref[...] = jnp.zeros_like(p_ref)

    qt = q_ref[...].astype(F32).T
    qt_b = qt.astype(BF16)
    km = kmean_ref[...]
    km_lane_head = lax.broadcasted_iota(jnp.int32, (nblk, hd2), 1) >> LOG2_HEAD_DIM
    qt_row_head = lax.broadcasted_iota(jnp.int32, (hd2, tq), 0) >> LOG2_HEAD_DIM
    nidx = lax.broadcasted_iota(jnp.int32, (nblk, tq), 0)
    qblk = 2 * j + (lax.broadcasted_iota(jnp.int32, (nblk, tq), 1) >> LOG2_MOBA_BLOCK)
    aux_row = lax.broadcasted_iota(jnp.int32, (hd2 - nblk, tq), 0)
    past = nidx < qblk

    qaug = []
    for hh in range(2):
        kmh = jnp.where(km_lane_head == hh, km, 0.0)
        kh, kl = _split_bf16(kmh)
        gate = _dot(kh, qt_b) + _dot(kl, qt_b)
        g = jnp.where(past, gate, -jnp.inf)
        rank = jnp.zeros((nblk, tq), F32)
        for m in range(nblk):
            gm = g[m:m + 1, :]
            rank = rank + jnp.where(m < nidx, jnp.where(gm >= g, 1.0, 0.0), jnp.where(gm > g, 1.0, 0.0))
        sel = (past & (rank < MOBA_TOPK)) | (nidx == qblk)
        s256 = s256_ref[0, hh:hh + 1, :]
        bias = jnp.where(sel, (nidx - qblk).astype(F32) * s256, MASK_BIAS)
        head_one = jnp.where(aux_row == hh, 1.0, 0.0)
        qaug.append(jnp.concatenate(
            [jnp.where(qt_row_head == hh, qt, 0.0), bias, head_one], axis=0).astype(BF16))
    qall = jnp.concatenate(qaug, axis=1)

    ncg = 4
    cg = [slice(c * blk, (c + 1) * blk) for c in range(ncg)]

    def softmax_c(st, m, c):
        mn = jnp.maximum(m, jnp.max(st, axis=0, keepdims=True))
        p_ref[:, cg[c]] = jnp.exp(st - mn).astype(BF16)
        return mn, jnp.exp(m - mn)

    def accumulate_c(n, alpha, c):
        off = pl.multiple_of(n * kt, kt)
        hh, a = c // 2, c % 2
        acc_ref[hh, :, a * blk:(a + 1) * blk] = (
            alpha * acc_ref[hh, :, a * blk:(a + 1) * blk]
            + _dot(vt_ref[hh, :, pl.ds(off, kt)], p_ref[:, cg[c]]))

    s_ref[...] = _dot(kaug_ref[0:kt, :], qall)
    acc_ref[...] = jnp.zeros_like(acc_ref)
    m0 = tuple(jnp.full((1, blk), -jnp.inf, F32) for _ in range(ncg))
    a0 = tuple(jnp.ones((1, blk), F32) for _ in range(ncg))

    def body(i, carry):
        ms, alphas = carry
        off = pl.multiple_of((i + 1) * kt, kt)
        ktile = kaug_ref[pl.ds(off, kt), :]
        new_m, new_a, nxt = [], [], []
        for c in range(ncg):
            nxt.append(_dot(ktile, qall[:, cg[c]]))
            accumulate_c(jnp.maximum(i - 1, 0), alphas[c], c)
            mn, al = softmax_c(s_ref[:, cg[c]], ms[c], c)
            new_m.append(mn)
            new_a.append(al)
        for c in range(ncg):
            s_ref[:, cg[c]] = nxt[c]
        return tuple(new_m), tuple(new_a)

    ms, alphas = lax.fori_loop(0, j, body, (m0, a0))
    for c in range(ncg):
        accumulate_c(jnp.maximum(j - 1, 0), alphas[c], c)
    key_i = lax.broadcasted_iota(jnp.int32, (blk, blk), 0)
    qry_i = lax.broadcasted_iota(jnp.int32, (blk, blk), 1)
    causal = key_i <= qry_i
    off = pl.multiple_of(j * kt, kt)
    for c in range(ncg):
        hh, a = c // 2, c % 2
        parts = [jnp.where(causal, s_ref[0:blk, cg[c]], -jnp.inf)] if a == 0 else [
            s_ref[0:blk, cg[c]], jnp.where(causal, s_ref[blk:kt, cg[c]], -jnp.inf)]
        mn = ms[c]
        for st in parts:
            mn = jnp.maximum(mn, jnp.max(st, axis=0, keepdims=True))
        for r, st in enumerate(parts):
            p_ref[r * blk:(r + 1) * blk, cg[c]] = jnp.exp(st - mn).astype(BF16)
        nk = len(parts) * blk
        acc_ref[hh, :, a * blk:(a + 1) * blk] = (
            jnp.exp(ms[c] - mn) * acc_ref[hh, :, a * blk:(a + 1) * blk]
            + _dot(vt_ref[hh, :, pl.ds(off, nk)], p_ref[0:nk, cg[c]]))

    for a in range(2):
        ot = []
        for hh in range(2):
            acc = acc_ref[hh, :, a * blk:(a + 1) * blk]
            ot.append(acc[0:HEAD_DIM] * (1.0 / acc[HEAD_DIM:HEAD_DIM + 1]))
        o_ref[a * blk:(a + 1) * blk, :] = jnp.concatenate(ot, axis=0).T.astype(o_ref.dtype)


def _moba_tables(nblk):
    slopes = 2.0 ** (-8.0 * np.arange(1, N_MOBA_HEADS + 1) / N_MOBA_HEADS)
    npair = N_MOBA_HEADS // 2
    kaux = np.zeros((npair, MOBA_BLOCK, 2 * HEAD_DIM), np.float32)
    s256 = np.zeros((npair, 8, 2 * MOBA_BLOCK), np.float32)
    for p in range(npair):
        for hh in range(2):
            kaux[p, :, nblk + hh] = slopes[2 * p + hh] * np.arange(MOBA_BLOCK)
            s256[p, hh, :] = slopes[2 * p + hh] * MOBA_BLOCK
    return jnp.asarray(kaux, BF16), jnp.asarray(s256, F32)


def _moba(q, k, v, bsz):
    npair, n, hd2 = q.shape
    t = n // bsz
    blk = MOBA_BLOCK
    nblk = t // blk
    assert blk == 1 << LOG2_MOBA_BLOCK and t % (2 * blk) == 0 and nblk % 8 == 0 and nblk <= HEAD_DIM
    assert hd2 == 2 * HEAD_DIM
    tq = 2 * blk
    steps = t // tq
    kaux, s256 = _moba_tables(nblk)
    q_spec = pl.BlockSpec((None, tq, hd2), lambda bi, p, i: (p, bi * steps + i, 0))
    kv_spec = pl.BlockSpec((None, t, hd2), lambda bi, p, i: (p, bi, 0))
    return pl.pallas_call(
        functools.partial(_moba_kernel, nblk=nblk, blk=blk),
        grid=(bsz, npair, steps),
        in_specs=[q_spec, kv_spec, kv_spec,
                  pl.BlockSpec((1, blk, hd2), lambda bi, p, i: (p, 0, 0)),
                  pl.BlockSpec((1, 8, tq), lambda bi, p, i: (p, 0, 0))],
        out_specs=q_spec,
        out_shape=jax.ShapeDtypeStruct((npair, n, hd2), BF16),
        scratch_shapes=[pltpu.VMEM((t, 2 * hd2), BF16),
                        pltpu.VMEM((2, VT_ROWS, t), BF16),
                        pltpu.VMEM((nblk, hd2), F32),
                        pltpu.VMEM((2 * blk, 2 * tq), F32),
                        pltpu.VMEM((2 * blk, 2 * tq), BF16),
                        pltpu.VMEM((2, VT_ROWS, tq), F32)],
        compiler_params=_params("parallel", "parallel", "arbitrary"),
        name="moba",
    )(q, k, v, kaux, s256)


def _gla_kernel(qk_ref, v_ref, go_ref, gl_ref, wg_ref, bg_ref, gn_ref, o_ref, st_ref,
                *, sub, chunk):
    kw = N_GLA_HEADS * GLA_DK
    vw = N_GLA_HEADS * GLA_DV
    tt = qk_ref.shape[1]

    @pl.when(pl.program_id(1) == 0)
    def _reset():
        st_ref[...] = jnp.zeros_like(st_ref)

    r = lax.broadcasted_iota(jnp.int32, (sub, sub), 0)
    c = lax.broadcasted_iota(jnp.int32, (sub, sub), 1)
    same = (r >> LOG2_HEAD_DIM) == (c >> LOG2_HEAD_DIM)
    causal = same & (c <= r)
    ltri = jnp.where(causal, 1.0, 0.0).astype(BF16)
    st_diag = ((lax.broadcasted_iota(jnp.int32, (vw, kw), 0) >> LOG2_HEAD_DIM)
               == (lax.broadcasted_iota(jnp.int32, (vw, kw), 1) >> LOG2_GLA_DK))
    grp = jnp.where((lax.broadcasted_iota(jnp.int32, (vw, vw), 0) >> LOG2_HEAD_DIM)
                    == (lax.broadcasted_iota(jnp.int32, (vw, vw), 1) >> LOG2_HEAD_DIM),
                    1.0, 0.0).astype(BF16)
    khead = lax.broadcasted_iota(jnp.int32, (sub, kw), 1) >> LOG2_GLA_DK
    vhead = lax.broadcasted_iota(jnp.int32, (sub, vw), 1) >> LOG2_HEAD_DIM

    for s in range(tt // sub):
        rows = slice(s * sub, (s + 1) * sub)
        z = _dot(gl_ref[0, rows, :], wg_ref[...]) + bg_ref[...]
        la = (jnp.minimum(z, 0.0) - jnp.log(1.0 + jnp.exp(-jnp.abs(z)))) * (1.0 / GLA_GATE_NORMALIZER)
        la_hi, la_lo = _split_bf16(la)
        b = _dot(ltri, la_hi) + _dot(ltri, la_lo)
        tot = [b[(ci + 1) * chunk - 1:(ci + 1) * chunk, :] for ci in range(sub // chunk)]
        bl = jnp.concatenate([jnp.broadcast_to(t, (chunk, kw)) for t in tot], axis=0)
        q = qk_ref[0, rows, 0:kw].astype(F32)
        k = qk_ref[0, rows, kw:2 * kw].astype(F32)
        v = v_ref[0, rows, :]
        qd = (q * (jnp.exp(b) * (GLA_DK ** -0.5))).astype(BF16)
        kd = (k * jnp.exp(-b)).astype(BF16)
        kend = (k * jnp.exp(bl - b)).astype(BF16)

        o = jnp.zeros((sub, vw), F32)
        for h in range(N_GLA_HEADS):
            a = _dot_nt(jnp.where(khead == h, qd, jnp.zeros_like(qd)), kd)
            a = jnp.where(causal, a, 0.0).astype(BF16)
            o = o + _dot(a, jnp.where(vhead == h, v, jnp.zeros_like(v)))

        inter = []
        for ci in range(sub // chunk):
            cr = slice(ci * chunk, (ci + 1) * chunk)
            st = st_ref[...]
            inter.append(_dot_nt(qd[cr], st.astype(BF16)))
            ut = _dot_tn(v[cr], kend[cr])
            st_ref[...] = st * jnp.exp(tot[ci]) + jnp.where(st_diag, ut, 0.0)
        o = o + jnp.concatenate(inter, axis=0)

        o2_hi, o2_lo = _split_bf16(o * o)
        ms = (_dot(o2_hi, grp) + _dot(o2_lo, grp)) * (1.0 / GLA_DV)
        go = go_ref[0, rows, :].astype(F32)
        y = (o * lax.rsqrt(ms + EPS)) * gn_ref[...] * (go * (1.0 / (1.0 + jnp.exp(-go))))
        o_ref[0, rows, :] = y.astype(o_ref.dtype)


def _gla(qk, v, go, gl, wg, bg, gn, l, tt):
    b, t, _ = qk.shape
    kw = N_GLA_HEADS * GLA_DK
    vw = N_GLA_HEADS * GLA_DV
    spec = lambda w: pl.BlockSpec((1, tt, w), lambda bi, i: (bi, i, 0))
    return pl.pallas_call(
        functools.partial(_gla_kernel, sub=4 * GLA_CHUNK, chunk=GLA_CHUNK),
        grid=(b, t // tt),
        in_specs=[spec(2 * kw), spec(vw), spec(vw), spec(LANES),
                  _layer(wg, l), _layer(bg, l), _layer(gn, l)],
        out_specs=spec(vw),
        out_shape=jax.ShapeDtypeStruct((b, t, vw), BF16),
        scratch_shapes=[pltpu.VMEM((vw, kw), F32)],
        compiler_params=_params("parallel", "arbitrary"),
        name="gla",
    )(qk, v, go, gl, wg, bg, gn)


def _mem_kv_kernel(mem_ref, g_ref, w_ref, k_ref, v_ref):
    nb, m, d = mem_ref.shape
    wd = k_ref.shape[-1]
    h = _rms(mem_ref[...].reshape(nb * m, d), g_ref[...]).astype(BF16)
    kv = _dot(h, w_ref[...].astype(BF16)).astype(BF16)
    k_ref[...] = kv[:, :wd].reshape(nb, m, wd)
    v_ref[...] = kv[:, wd:].reshape(nb, m, wd)


def _mem_kv(mem, g, wkv):
    b, m, d = mem.shape
    nl = wkv.shape[0]
    wd = wkv.shape[-1] // 2
    out = jax.ShapeDtypeStruct((nl, b, m, wd), BF16)
    ospec = pl.BlockSpec((None, b, m, wd), lambda l: (l, 0, 0, 0))
    return pl.pallas_call(
        _mem_kv_kernel,
        grid=(nl,),
        in_specs=[pl.BlockSpec((b, m, d), lambda l: (0, 0, 0)),
                  pl.BlockSpec((None, 1, d), lambda l: (l, 0, 0)),
                  pl.BlockSpec((None, d, 2 * wd), lambda l: (l, 0, 0))],
        out_specs=[ospec, ospec],
        out_shape=[out, out],
        compiler_params=_params("parallel"),
        name="mem_kv",
    )(mem, g, wkv)


def _mix_xattn_kernel(x_ref, c3_ref, c3h_ref, ym_ref, yg_ref, wout_f32, cw_ref, cb_ref,
                      gx_ref, wq_f32, mk_ref, mv_ref, wo_f32, o_ref, wout_ref, wq_ref, wo_ref,
                      *, tiles_per_seq):
    tm = x_ref.shape[0]

    @pl.when(pl.program_id(0) == 0)
    def _cast_weights():
        wout_ref[...] = wout_f32[...].astype(BF16)
        wq_ref[...] = (wq_f32[...] * (HEAD_DIM ** -0.5)).astype(BF16)
        wo_ref[...] = wo_f32[...].astype(BF16)

    cc = cw_ref.shape[-1]
    nh = N_XATTN_HEADS
    first = (pl.program_id(0) % tiles_per_seq) == 0

    c3 = c3_ref[...].astype(F32)
    u = c3[:, cc:2 * cc] * c3[:, 2 * cc:3 * cc]
    halo = c3h_ref[...].astype(F32)
    uh = halo[:, cc:2 * cc] * halo[:, 2 * cc:3 * cc]
    uh = jnp.where(first, 0.0, uh)
    last1 = uh[BF16_SUBLANES - 1:BF16_SUBLANES, :]
    last2 = uh[BF16_SUBLANES - 2:BF16_SUBLANES - 1, :]
    row = lax.broadcasted_iota(jnp.int32, (tm, cc), 0)
    u1 = jnp.where(row == 0, last1, pltpu.roll(u, 1, 0))
    u2 = jnp.where(row == 0, last2, jnp.where(row == 1, last1, pltpu.roll(u, 2, 0)))
    cw = cw_ref[...]
    yc = c3[:, 0:cc] * (cw[0:1] * u2 + cw[1:2] * u1 + cw[2:3] * u + cb_ref[...])

    y = jnp.concatenate([yc.astype(BF16)] + [ym_ref[p] for p in range(ym_ref.shape[0])]
                        + [yg_ref[...]], axis=1)
    x1 = x_ref[...] + _dot(y, wout_ref[...])

    h = _rms(x1, gx_ref[...]).astype(BF16)
    q = _dot(h, wq_ref[...]).astype(BF16)
    k = mk_ref[...]
    v = mv_ref[...]
    qhead = lax.broadcasted_iota(jnp.int32, q.shape, 1) >> LOG2_HEAD_DIM
    vhead = lax.broadcasted_iota(jnp.int32, v.shape, 1) >> LOG2_HEAD_DIM
    o = jnp.zeros(q.shape, F32)
    for hd in range(nh):
        s = _dot_nt(jnp.where(qhead == hd, q, jnp.zeros_like(q)), k)
        p = jnp.exp(s - jnp.max(s, axis=-1, keepdims=True))
        l = jnp.sum(p, axis=-1, keepdims=True)
        pv = _dot(p.astype(BF16), jnp.where(vhead == hd, v, jnp.zeros_like(v)))
        o = o + pv * (1.0 / l)
    o_ref[...] = x1 + _dot(o.astype(BF16), wo_ref[...])


def _mix_xattn(x, c3, ym, yg, wout, cw, cb, gx, wq, memk, memv, wo, l, *, tm, seq):
    n, d = x.shape
    tiles_per_seq = seq // tm
    hb = tm // BF16_SUBLANES
    row = lambda w: pl.BlockSpec((tm, w), lambda i: (i, 0))
    mem_spec = pl.BlockSpec((None, None) + memk.shape[2:], lambda i: (l, i // tiles_per_seq, 0, 0))
    return pl.pallas_call(
        functools.partial(_mix_xattn_kernel, tiles_per_seq=tiles_per_seq),
        grid=(n // tm,),
        in_specs=[row(d), row(c3.shape[1]),
                  pl.BlockSpec((BF16_SUBLANES, c3.shape[1]),
                               lambda i: (jnp.maximum(i * hb - 1, 0), 0)),
                  pl.BlockSpec((ym.shape[0], tm, ym.shape[2]), lambda i: (0, i, 0)),
                  row(yg.shape[1]),
                  _layer(wout, l), _layer(cw, l), _layer(cb, l), _layer(gx, l), _layer(wq, l),
                  mem_spec, mem_spec, _layer(wo, l)],
        out_specs=row(d),
        out_shape=jax.ShapeDtypeStruct((n, d), F32),
        scratch_shapes=[pltpu.VMEM(wout.shape[1:], BF16), pltpu.VMEM(wq.shape[1:], BF16),
                        pltpu.VMEM(wo.shape[1:], BF16)],
        compiler_params=_params("arbitrary"),
        name="mix_xattn",
    )(x, c3, c3, ym, yg, wout, cw, cb, gx, wq, memk, memv, wo)


def _ffn_kernel(x_ref, g_ref, wa_ref, wg_ref, cwa_ref, cwg_ref, cba_ref, cbg_ref, wd_ref,
                gf_ref, o_ref, hext_ref, hid_ref, ua_ref, ug_ref, *, tiles_per_seq, final):
    tm = x_ref.shape[0]
    halo = BF16_SUBLANES
    tn = ua_ref.shape[1]
    nc = wa_ref.shape[1] // tn
    first = (pl.program_id(0) % tiles_per_seq) == 0

    def cols(j):
        return pl.ds(pl.multiple_of(j * tn, tn), tn)

    @pl.when(first)
    def _zero_halo():
        hext_ref[0:halo, :] = jnp.zeros((halo, hext_ref.shape[1]), BF16)

    @pl.when(jnp.logical_not(first))
    def _carry_halo():
        hext_ref[0:halo, :] = hext_ref[tm:tm + halo, :]

    x = x_ref[...]
    hext_ref[halo:, :] = _rms(x, g_ref[...]).astype(BF16)

    def up(j):
        hext = hext_ref[...]
        return _dot(hext, wa_ref[:, cols(j)]), _dot(hext, wg_ref[:, cols(j)])

    def conv(u, cw, cb):
        y = cw[0:1] * pltpu.roll(u, 2, 0) + cw[1:2] * pltpu.roll(u, 1, 0) + cw[2:3] * u + cb
        return y[F32_SUBLANES:]

    def gate(j):
        src = slice(halo - F32_SUBLANES, halo + tm)
        a = conv(ua_ref[src, :], cwa_ref[:, cols(j)], cba_ref[:, cols(j)])
        g = conv(ug_ref[src, :], cwg_ref[:, cols(j)], cbg_ref[:, cols(j)])
        h2 = (0.5 * a).astype(BF16)
        hid_ref[:, cols(j)] = (h2 + h2 * jnp.tanh(h2)) * g.astype(BF16)

    ua_ref[...], ug_ref[...] = up(0)

    def body(j, carry):
        ua, ug = up(j + 1)
        gate(j)
        ua_ref[...] = ua
        ug_ref[...] = ug
        return carry

    lax.fori_loop(0, nc - 1, body, 0)
    gate(nc - 1)
    y = x + _dot(hid_ref[...], wd_ref[...])
    if final:
        y = _rms(y, gf_ref[...])
    o_ref[...] = y


def _ffn(x, g, wa, wg, cwa, cwg, cba, cbg, wd, gf, l, *, tm, tn, seq, final):
    n, d = x.shape
    return pl.pallas_call(
        functools.partial(_ffn_kernel, tiles_per_seq=seq // tm, final=final),
        grid=(n // tm,),
        in_specs=[pl.BlockSpec((tm, d), lambda i: (i, 0)), _layer(g, l),
                  _layer(wa, l), _layer(wg, l), _layer(cwa, l), _layer(cwg, l),
                  _layer(cba, l), _layer(cbg, l), _layer(wd, l), _resident(gf.shape)],
        out_specs=pl.BlockSpec((tm, d), lambda i: (i, 0)),
        out_shape=jax.ShapeDtypeStruct((n, d), F32),
        scratch_shapes=[pltpu.VMEM((tm + BF16_SUBLANES, d), BF16),
                        pltpu.VMEM((tm, wa.shape[-1]), BF16),
                        pltpu.VMEM((tm + BF16_SUBLANES, tn), F32),
                        pltpu.VMEM((tm + BF16_SUBLANES, tn), F32)],
        compiler_params=_params("arbitrary"),
        name="ffn",
    )(x, g, wa, wg, cwa, cwg, cba, cbg, wd, gf)


def _pad_cols(w, pad_to):
    return jnp.pad(w, [(0, 0)] * (w.ndim - 1) + [(0, pad_to - w.shape[-1])])


def _split_up_kernel(w_ref, a_ref, g_ref, *, d_ff):
    w = w_ref[...]
    zeros = jnp.zeros((w.shape[0], a_ref.shape[-1] - d_ff), BF16)
    a_ref[:, 0:d_ff] = w[:, 0:d_ff].astype(BF16)
    a_ref[:, d_ff:] = zeros
    g_ref[:, 0:d_ff] = w[:, d_ff:2 * d_ff].astype(BF16)
    g_ref[:, d_ff:] = zeros


def _split_up(w, d_ff, ff_pad, rows):
    nl, d, _ = w.shape
    out = jax.ShapeDtypeStruct((nl, d, ff_pad), BF16)
    ospec = pl.BlockSpec((None, rows, ff_pad), lambda l, i: (l, i, 0))
    return pl.pallas_call(
        functools.partial(_split_up_kernel, d_ff=d_ff),
        grid=(nl, d // rows),
        in_specs=[pl.BlockSpec((None, rows, 2 * d_ff), lambda l, i: (l, i, 0))],
        out_specs=[ospec, ospec],
        out_shape=[out, out],
        compiler_params=_params("parallel", "parallel"),
        name="ffn_up_prep",
    )(w)


def _pad_down_kernel(w_ref, o_ref):
    rows = w_ref.shape[0]
    o_ref[0:rows, :] = w_ref[...].astype(BF16)
    o_ref[rows:, :] = jnp.zeros((o_ref.shape[0] - rows, o_ref.shape[1]), BF16)


def _pad_down(w, ff_pad):
    nl, d_ff, d = w.shape
    return pl.pallas_call(
        _pad_down_kernel,
        grid=(nl,),
        in_specs=[pl.BlockSpec((None, d_ff, d), lambda l: (l, 0, 0))],
        out_specs=pl.BlockSpec((None, ff_pad, d), lambda l: (l, 0, 0)),
        out_shape=jax.ShapeDtypeStruct((nl, ff_pad, d), BF16),
        compiler_params=_params("parallel"),
        name="ffn_down_prep",
    )(w)


def kernel(x, mem, norm_mix_g, w_in, sc_conv_w, sc_conv_b, gla_w_gate, gla_b_gate, gla_norm_g,
           w_out, norm_xattn_g, norm_mem_g, xattn_wq, xattn_wkv, xattn_wo, norm_ffn_g,
           ffn_w_up, ffn_conv_w, ffn_conv_b, ffn_w_down, final_norm_g):
    bsz, seq, d = x.shape
    nl = w_in.shape[0]
    n = bsz * seq
    conv_ch = sc_conv_w.shape[-1]
    moba_w = N_MOBA_HEADS * HEAD_DIM
    gk_w = N_GLA_HEADS * GLA_DK
    gv_w = N_GLA_HEADS * GLA_DV
    d_ff = ffn_w_down.shape[1]
    ff_pad = -(-d_ff // FFN_CHUNK) * FFN_CHUNK
    tm = TOKEN_TILE

    widths = (3 * conv_ch, moba_w, moba_w, moba_w, 2 * gk_w, gv_w, gv_w, LANES)
    pair_major = (False, True, True, True, False, False, False, False)
    in_scales = (1.0, HEAD_DIM ** -0.5, 1.0, 1.0, 1.0, 1.0, 1.0)
    w_in_t = jnp.swapaxes(w_in, 1, 2)
    wg_p = jnp.pad(gla_w_gate.astype(BF16), ((0, 0), (LANES - GLA_GATE_RANK, 0), (0, 0)))
    gn_t = jnp.tile(gla_norm_g, (1, N_GLA_HEADS))
    wa, wgt = _split_up(ffn_w_up, d_ff, ff_pad, rows=FFN_CHUNK)
    cwa = _pad_cols(ffn_conv_w[..., :d_ff], ff_pad)
    cwg = _pad_cols(ffn_conv_w[..., d_ff:], ff_pad)
    cba = _pad_cols(ffn_conv_b[:, None, :d_ff], ff_pad)
    cbg = _pad_cols(ffn_conv_b[:, None, d_ff:], ff_pad)
    wd = _pad_down(ffn_w_down, ff_pad)

    memk, memv = _mem_kv(mem, norm_mem_g[:, None, :], xattn_wkv)

    vec = lambda a: a[:, None, :]
    norm_mix_v, norm_xattn_v, norm_ffn_v = vec(norm_mix_g), vec(norm_xattn_g), vec(norm_ffn_g)
    gla_bg_v, gn_v, sc_conv_b_v = vec(gla_b_gate), vec(gn_t), vec(sc_conv_b)

    xf = x.reshape(n, d)
    for l in range(nl):
        c3, mq, mk, mv, gqk, gv, go, gl = _norm_inproj(
            xf, norm_mix_v, w_in_t, l, widths, pair_major, in_scales, tm)
        r3 = lambda a: a.reshape(bsz, seq, a.shape[-1])
        ym = _moba(mq, mk, mv, bsz)
        yg = _gla(r3(gqk), r3(gv), r3(go), r3(gl), wg_p, gla_bg_v, gn_v, l,
                  tt=GLA_TILE).reshape(n, gv_w)
        xf = _mix_xattn(xf, c3, ym, yg, w_out, sc_conv_w, sc_conv_b_v, norm_xattn_v, xattn_wq,
                        memk, memv, xattn_wo, l, tm=tm, seq=seq)
        xf = _ffn(xf, norm_ffn_v, wa, wgt, cwa, cwg, cba, cbg, wd, final_norm_g[None, :], l,
                  tm=tm, tn=FFN_CHUNK, seq=seq, final=(l == nl - 1))
    return xf.reshape(bsz, seq, d)
```
